```python
import math
import jax, jax.numpy as jnp
from jax import lax
import numpy as np

D_MODEL = 1024
BATCH = 8
SEQ = 2048
DEPTH = 4
DEC_BATCH = 8
DEC_SEQ = 4096
PAST_LEN = 128

HEAD_DIM = 64
RWKV_WIDTH = D_MODEL // 2
RWKV_HEADS = RWKV_WIDTH // HEAD_DIM
ATTN_WIDTH = D_MODEL - RWKV_WIDTH
ATTN_HEADS = ATTN_WIDTH // HEAD_DIM
KV_HEADS = 2
KV_WIDTH = KV_HEADS * HEAD_DIM
Q_PER_KV = ATTN_HEADS // KV_HEADS
DECAY_LORA = 32
ICLR_LORA = 32
GATE_LORA = 64
D_FF = int(math.ceil(8 * D_MODEL / 3 / 256)) * 256
N_META = 16
GRID_W = 64
ROPE_THETA = 10000.0
ROPE_FREQS = HEAD_DIM // 4
BLOCK_Q = 128
DEEPNORM_ALPHA = (2.0 * DEPTH) ** 0.25
DEEPNORM_BETA = (8.0 * DEPTH) ** -0.25
LN_EPS = 1e-5
GN_EPS = 64e-5
QK_EPS = 1e-6
R_END = RWKV_WIDTH
K_END = 2 * RWKV_WIDTH
V_END = 3 * RWKV_WIDTH
WLO_END = V_END + DECAY_LORA
ALO_END = WLO_END + ICLR_LORA
RWKV_IN_WIDTH = ALO_END + GATE_LORA
Q_END = RWKV_IN_WIDTH + ATTN_WIDTH
AK_END = Q_END + KV_WIDTH
IN_WIDTH = AK_END + KV_WIDTH

kernel_name = "hymba_rwkv7_axial_gqa_deepnorm_encoder"

F32 = jnp.float32


def _heads(t):
    return t.reshape(t.shape[:-1] + (-1, HEAD_DIM))


def layer_norm(x, g, b):
    xf = x.astype(F32)
    mu = jnp.mean(xf, axis=-1, keepdims=True)
    var = jnp.mean(jnp.square(xf - mu), axis=-1, keepdims=True)
    y = (xf - mu) * lax.rsqrt(var + LN_EPS) * g.astype(F32) + b.astype(F32)
    return y.astype(x.dtype)


def token_shift_centred(p, mu_prev, mu_next):
    prev = jnp.pad(p[:, :-1], ((0, 0), (1, 0), (0, 0)))
    nxt = jnp.pad(p[:, 1:], ((0, 0), (0, 1), (0, 0)))
    return p + mu_prev * (prev - p) + mu_next * (nxt - p)


def _wkv_step(S, xs):
    r_t, w_t, k_t, v_t, kk_t, a_t = xs
    sa = jnp.einsum('zbhvk,zbhk->zbhv', S, -kk_t)
    S = (S * w_t[..., None, :]
         + sa[..., :, None] * (kk_t * a_t)[..., None, :]
         + v_t[..., :, None] * k_t[..., None, :])
    o = jnp.einsum('zbhvk,zbhk->zbhv', S, r_t)
    return S, o


def rwkv7_bidir(p, mu, w0, w_up, a0, a_up, g_up, k_k, k_a, r_k, gn_g, gn_b):
    dtype = p.dtype
    p = p.astype(F32)
    B, T, _ = p.shape
    mu = mu.astype(F32)
    p = token_shift_centred(p, mu[0], mu[1])
    r, k, v, w_lo, a_lo, g_lo = jnp.split(p, [R_END, K_END, V_END, WLO_END, ALO_END], axis=-1)
    w = w0.astype(F32)[:, None, None, :] + jnp.einsum('btr,zrc->zbtc', jnp.tanh(w_lo), w_up.astype(F32))
    decay = jnp.exp(-jnp.exp(-jax.nn.softplus(-w) - 0.5))
    a = jax.nn.sigmoid(a0.astype(F32)[:, None, None, :]
                       + jnp.einsum('btr,zrc->zbtc', a_lo, a_up.astype(F32)))
    g = jax.nn.sigmoid(g_lo) @ g_up.astype(F32)
    kk = _heads(k * k_k.astype(F32))
    kk = kk / jnp.maximum(jnp.sqrt(jnp.sum(kk * kk, axis=-1, keepdims=True)), 1e-12)
    k_dir = k[None] * (1.0 + (a - 1.0) * k_a.astype(F32))
    rh, vh = _heads(r), _heads(v)

    def both(t):
        return jnp.broadcast_to(t[None], (2,) + t.shape)

    seqs = (both(rh), _heads(decay), _heads(k_dir), both(vh), both(kk), _heads(a))
    seqs = tuple(jnp.moveaxis(jnp.stack([t[0], jnp.flip(t[1], axis=1)]), 2, 0) for t in seqs)
    S0 = jnp.zeros((2, B, RWKV_HEADS, HEAD_DIM, HEAD_DIM), F32)
    _, o = lax.scan(_wkv_step, S0, seqs)
    o = jnp.moveaxis(o, 0, 2)
    o = o[0] + jnp.flip(o[1], axis=1)
    mo = jnp.mean(o, axis=-1, keepdims=True)
    vo = jnp.mean(jnp.square(o - mo), axis=-1, keepdims=True)
    o = (o - mo) * lax.rsqrt(vo + GN_EPS) * _heads(gn_g.astype(F32)) + _heads(gn_b.astype(F32))
    bonus = jnp.sum(rh * _heads(k_dir[0] + k_dir[1]) * r_k.astype(F32), axis=-1, keepdims=True) * vh
    out = (o + bonus).reshape(B, T, RWKV_WIDTH) * g
    return out.astype(dtype)


def axial_rope_tables(n_tokens):
    rows = n_tokens // GRID_W
    row_idx = jnp.repeat(jnp.arange(rows), GRID_W)
    col_idx = jnp.tile(jnp.arange(GRID_W), rows)
    pos = jnp.stack([row_idx, col_idx], axis=-1).astype(F32)
    inv_freq = ROPE_THETA ** (-jnp.arange(ROPE_FREQS, dtype=F32) / ROPE_FREQS)
    ang = pos[:, :, None] * inv_freq
    ang = jnp.concatenate([jnp.zeros((N_META, 2, ROPE_FREQS), F32), ang], axis=0)
    return jnp.cos(ang), jnp.sin(ang)


def apply_axial_rope(x, cos, sin):
    xr = x.reshape(x.shape[:-1] + (2, 2, ROPE_FREQS))
    x1, x2 = xr[..., 0, :], xr[..., 1, :]
    c = cos[None, :, None]
    s = sin[None, :, None]
    out = jnp.stack([x1 * c - x2 * s, x2 * c + x1 * s], axis=-2)
    return out.reshape(x.shape)


def rms_heads(x, g):
    return x * lax.rsqrt(jnp.mean(x * x, axis=-1, keepdims=True) + QK_EPS) * g.astype(F32)


def attend_block(qb, k, v):
    s = jnp.einsum('bqhgd,bkhd->bhgqk', qb, k) * (HEAD_DIM ** -0.5)
    pr = jax.nn.softmax(s, axis=-1)
    return jnp.einsum('bhgqk,bkhd->bqhgd', pr, v)


def axial_gqa(pq, pk, pv, q_gain, k_gain, cos, sin):
    dtype = pq.dtype
    B, T, _ = pq.shape
    q = apply_axial_rope(rms_heads(_heads(pq.astype(F32)), q_gain), cos, sin)
    k = apply_axial_rope(rms_heads(_heads(pk.astype(F32)), k_gain), cos, sin)
    v = _heads(pv.astype(F32))
    q = q.reshape(B, T, KV_HEADS, Q_PER_KV, HEAD_DIM)
    meta_out = attend_block(q[:, :N_META], k, v)
    n_blk = (T - N_META) // BLOCK_Q
    qr = q[:, N_META:].reshape(B, n_blk, BLOCK_Q, KV_HEADS, Q_PER_KV, HEAD_DIM)
    qr = jnp.moveaxis(qr, 1, 0)
    real_out = lax.map(lambda qb: attend_block(qb, k, v), qr)
    real_out = jnp.moveaxis(real_out, 0, 1).reshape(B, T - N_META, KV_HEADS, Q_PER_KV, HEAD_DIM)
    out = jnp.concatenate([meta_out, real_out], axis=1)
    return out.reshape(B, T, ATTN_WIDTH).astype(dtype)


def encoder_layer(x, lp, cos, sin):
    proj = x @ lp['w_in']
    y_rwkv = rwkv7_bidir(proj[..., :RWKV_IN_WIDTH], lp['shift_mu'], lp['decay_w0'], lp['decay_up'],
                         lp['iclr_a0'], lp['iclr_up'], lp['gate_up'], lp['k_k'], lp['k_a'],
                         lp['r_k'], lp['gn_g'], lp['gn_b'])
    y_attn = axial_gqa(proj[..., RWKV_IN_WIDTH:Q_END], proj[..., Q_END:AK_END], proj[..., AK_END:],
                       lp['q_gain'], lp['k_gain'], cos, sin)
    mix = jnp.concatenate([y_rwkv, y_attn], axis=-1) @ lp['w_out']
    x = layer_norm(DEEPNORM_ALPHA * x + mix, lp['ln1_g'], lp['ln1_b'])
    gate, up = jnp.split(x @ lp['w_ffn_in'], 2, axis=-1)
    ffn = (jax.nn.silu(gate) * up) @ lp['w_ffn_out']
    return layer_norm(DEEPNORM_ALPHA * x + ffn, lp['ln2_g'], lp['ln2_b'])


def run_trunk(x, meta_tokens, ln_in_g, ln_in_b, params):
    B, n_tok, _ = x.shape
    cos, sin = axial_rope_tables(n_tok)
    meta = jnp.broadcast_to(meta_tokens[None].astype(x.dtype), (B, N_META, D_MODEL))
    h = layer_norm(jnp.concatenate([meta, x], axis=1), ln_in_g, ln_in_b)
    for l in range(DEPTH):
        lp = {name: arr[l] for name, arr in params.items()}
        h = encoder_layer(h, lp, cos, sin)
    return h[:, N_META:]


def setup_inputs(seed: int = 0) -> dict:
    key = jax.random.key(seed)
    ks = jax.random.split(key, 32)
    nrm = jax.random.normal
    L = DEPTH
    return {
        "x_prompt": nrm(ks[0], (BATCH, SEQ, D_MODEL), F32),
        "x_sample": nrm(ks[1], (DEC_BATCH, DEC_SEQ, D_MODEL), F32),
        "meta_tokens": nrm(ks[2], (N_META, D_MODEL), F32),
        "ln_in_g": 1.0 + 0.02 * nrm(ks[3], (D_MODEL,), F32),
        "ln_in_b": 0.02 * nrm(ks[4], (D_MODEL,), F32),
        "w_in": nrm(ks[5], (L, D_MODEL, IN_WIDTH), F32) * D_MODEL ** -0.5,
        "shift_mu": jax.random.uniform(ks[6], (L, 2, RWKV_IN_WIDTH), F32, 0.0, 0.5),
        "decay_w0": jax.random.uniform(ks[7], (L, 2, RWKV_WIDTH), F32, -6.0, 0.0),
        "decay_up": nrm(ks[8], (L, 2, DECAY_LORA, RWKV_WIDTH), F32) * DECAY_LORA ** -0.5,
        "iclr_a0": 0.1 * nrm(ks[9], (L, 2, RWKV_WIDTH), F32),
        "iclr_up": nrm(ks[10], (L, 2, ICLR_LORA, RWKV_WIDTH), F32) * ICLR_LORA ** -0.5,
        "gate_up": nrm(ks[11], (L, GATE_LORA, RWKV_WIDTH), F32) * GATE_LORA ** -0.5,
        "k_k": 0.85 + 0.05 * nrm(ks[12], (L, RWKV_WIDTH), F32),
        "k_a": 1.0 + 0.05 * nrm(ks[13], (L, RWKV_WIDTH), F32),
        "r_k": 0.1 * nrm(ks[14], (L, RWKV_HEADS, HEAD_DIM), F32),
        "gn_g": 1.0 + 0.02 * nrm(ks[15], (L, RWKV_WIDTH), F32),
        "gn_b": 0.02 * nrm(ks[16], (L, RWKV_WIDTH), F32),
        "q_gain": 1.0 + 0.02 * nrm(ks[17], (L, HEAD_DIM), F32),
        "k_gain": 1.0 + 0.02 * nrm(ks[18], (L, HEAD_DIM), F32),
        "w_out": nrm(ks[19], (L, D_MODEL, D_MODEL), F32) * (D_MODEL ** -0.5 * DEEPNORM_BETA),
        "ln1_g": 1.0 + 0.02 * nrm(ks[20], (L, D_MODEL), F32),
        "ln1_b": 0.02 * nrm(ks[21], (L, D_MODEL), F32),
        "w_ffn_in": nrm(ks[22], (L, D_MODEL, 2 * D_FF), F32) * D_MODEL ** -0.5,
        "w_ffn_out": nrm(ks[23], (L, D_FF, D_MODEL), F32) * (D_FF ** -0.5 * DEEPNORM_BETA),
        "ln2_g": 1.0 + 0.02 * nrm(ks[24], (L, D_MODEL), F32),
        "ln2_b": 0.02 * nrm(ks[25], (L, D_MODEL), F32),
    }


def reference(x_prompt, x_sample, meta_tokens, ln_in_g, ln_in_b, w_in, shift_mu, decay_w0, decay_up,
              iclr_a0, iclr_up, gate_up, k_k, k_a, r_k, gn_g, gn_b, q_gain, k_gain, w_out,
              ln1_g, ln1_b, w_ffn_in, w_ffn_out, ln2_g, ln2_b):
    params = dict(w_in=w_in, shift_mu=shift_mu, decay_w0=decay_w0, decay_up=decay_up,
                  iclr_a0=iclr_a0, iclr_up=iclr_up, gate_up=gate_up, k_k=k_k, k_a=k_a, r_k=r_k,
                  gn_g=gn_g, gn_b=gn_b, q_gain=q_gain, k_gain=k_gain, w_out=w_out,
                  ln1_g=ln1_g, ln1_b=ln1_b, w_ffn_in=w_ffn_in, w_ffn_out=w_ffn_out,
                  ln2_g=ln2_g, ln2_b=ln2_b)
    y_prompt = run_trunk(x_prompt, meta_tokens, ln_in_g, ln_in_b, params)
    y_sample = run_trunk(x_sample, meta_tokens, ln_in_g, ln_in_b, params)
    return (y_prompt, y_sample)
```

```python
import functools
import math

import jax
import jax.numpy as jnp
from jax import lax
from jax.experimental import pallas as pl
from jax.experimental.pallas import tpu as pltpu

F32 = jnp.float32
BF16 = jnp.bfloat16

D_MODEL = 1024
DEPTH = 4
HEAD_DIM = 64
RWKV_WIDTH = 512
RWKV_HEADS = 8
ATTN_WIDTH = 512
KV_WIDTH = 128
LORA_WIDTH = 128
RWKV_IN_WIDTH = 3 * RWKV_WIDTH + LORA_WIDTH
IN_WIDTH = RWKV_IN_WIDTH + ATTN_WIDTH + 2 * KV_WIDTH
D_FF = 2816
FF_CHUNK = 1408
N_META = 16
GRID_W = 64
ROPE_THETA = 10000.0
ROPE_FREQS = 16
DEEPNORM_ALPHA = (2.0 * DEPTH) ** 0.25
LN_EPS = 1e-5
GN_EPS = 64e-5
QK_EPS = 1e-6
DECAY_SCALE = math.exp(-0.5)

LANES = 128
CHUNK = 64
TAIL = 64
PAD = TAIL - N_META
ROW_TILE = 512
Q_TILE = 256
K_PREP_ROWS = 512
NEG_BIG = -1e30
VMEM_LIMIT = 56 * 1024 * 1024


def _dot(a, b):
    return jnp.dot(a, b, preferred_element_type=F32)


def _dot_nt(a, b):
    return lax.dot_general(a, b, (((1,), (1,)), ((), ())), preferred_element_type=F32)


def _split3(x):
    hi = x.astype(BF16)
    r1 = x - hi.astype(F32)
    mid = r1.astype(BF16)
    lo = (r1 - mid.astype(F32)).astype(BF16)
    return hi, mid, lo


def _dot3_right(x, m):
    hi, mid, lo = _split3(x)
    return _dot(hi, m) + _dot(mid, m) + _dot(lo, m)


def _dot3_left(m, x):
    hi, mid, lo = _split3(x)
    return _dot(m, hi) + _dot(m, mid) + _dot(m, lo)


def _layer_norm(x, g, b):
    mu = jnp.mean(x, axis=-1, keepdims=True)
    xc = x - mu
    var = jnp.mean(xc * xc, axis=-1, keepdims=True)
    return xc * lax.rsqrt(var + LN_EPS) * g + b


def _sigmoid(x):
    return 1.0 / (1.0 + jnp.exp(-x))


def _pick_tile(total, pref):
    t = pref
    while total % t:
        t //= 2
    return t


def _embed_kernel(x_ref, meta_ref, g_ref, b_ref, o_ref, *, n_x_tiles):
    i = pl.program_id(1)

    @pl.when(i < n_x_tiles)
    def _():
        o_ref[0] = _layer_norm(x_ref[0], g_ref[...], b_ref[...])

    @pl.when(i == n_x_tiles)
    def _():
        o_ref[0] = jnp.zeros(o_ref.shape[1:], F32)
        o_ref[0, PAD:TAIL, :] = _layer_norm(meta_ref[...], g_ref[...], b_ref[...])


def _embed(x, meta, g, b):
    B, N, D = x.shape
    te = _pick_tile(N, ROW_TILE)
    n_x_tiles = N // te
    assert te >= TAIL
    return pl.pallas_call(
        functools.partial(_embed_kernel, n_x_tiles=n_x_tiles),
        grid=(B, n_x_tiles + 1),
        in_specs=[
            pl.BlockSpec((1, te, D), lambda bi, i: (bi, jnp.minimum(i, n_x_tiles - 1), 0)),
            pl.BlockSpec((N_META, D), lambda bi, i: (0, 0)),
            pl.BlockSpec((1, D), lambda bi, i: (0, 0)),
            pl.BlockSpec((1, D), lambda bi, i: (0, 0)),
        ],
        out_specs=pl.BlockSpec((1, te, D), lambda bi, i: (bi, i, 0)),
        out_shape=jax.ShapeDtypeStruct((B, N + TAIL, D), F32),
        compiler_params=pltpu.CompilerParams(dimension_semantics=("arbitrary", "arbitrary")),
        name="embed",
    )(x, meta, g.reshape(1, D), b.reshape(1, D))


def _proj_kernel(h_ref, w_ref, pr_ref, q_ref, kv_ref):
    y = _dot(h_ref[...].astype(BF16), w_ref[...])
    pr_ref[...] = y[:, :RWKV_IN_WIDTH]
    q_ref[...] = y[:, RWKV_IN_WIDTH:RWKV_IN_WIDTH + ATTN_WIDTH]
    kv_ref[...] = y[:, RWKV_IN_WIDTH + ATTN_WIDTH:]


def _proj(h2, w_in_b):
    M, D = h2.shape
    tm = _pick_tile(M, ROW_TILE)
    return pl.pallas_call(
        _proj_kernel,
        grid=(M // tm,),
        in_specs=[
            pl.BlockSpec((tm, D), lambda i: (i, 0)),
            pl.BlockSpec((D, IN_WIDTH), lambda i: (0, 0)),
        ],
        out_specs=[
            pl.BlockSpec((tm, RWKV_IN_WIDTH), lambda i: (i, 0)),
            pl.BlockSpec((tm, ATTN_WIDTH), lambda i: (i, 0)),
            pl.BlockSpec((tm, 2 * KV_WIDTH), lambda i: (i, 0)),
        ],
        out_shape=[
            jax.ShapeDtypeStruct((M, RWKV_IN_WIDTH), F32),
            jax.ShapeDtypeStruct((M, ATTN_WIDTH), F32),
            jax.ShapeDtypeStruct((M, 2 * KV_WIDTH), F32),
        ],
        compiler_params=pltpu.CompilerParams(
            dimension_semantics=("arbitrary",), vmem_limit_bytes=VMEM_LIMIT),
        name="proj",
    )(h2, w_in_b)


def _rwkv_kernel(pr_ref, prev_ref, next_ref, mu_ref, w0_ref, w1_ref, a0_ref, w2_ref, w3_ref,
                 kk_ref, ka_ref, rk_ref, gg_ref, gb_ref, ones_ref,
                 out_ref, hs_ref, ob_ref, *, n_chunks):
    L = CHUNK
    pz = pl.program_id(1)
    c = pl.program_id(2)
    z = 1 - pz
    tau = jnp.where(pz == 1, c, n_chunks - 1 - c)
    j = lax.rem(tau + n_chunks - 1, n_chunks)

    @pl.when(c == 0)
    def _():
        hs_ref[...] = jnp.zeros(hs_ref.shape, F32)

    row = lax.broadcasted_iota(jnp.int32, (L, 1), 0)
    valid = jnp.logical_or(j != n_chunks - 1, row >= PAD)

    p = jnp.where(valid, pr_ref[0], 0.0)
    p_before = prev_ref[0, 7:8, :]
    p_after = jnp.where(j == n_chunks - 2, 0.0, next_ref[0, 0:1, :])
    prev = jnp.where(row == 0, p_before, pltpu.roll(p, 1, axis=0))
    nxt = jnp.where(row == L - 1, p_after, pltpu.roll(p, L - 1, axis=0))
    mu = mu_ref[...]
    ps = p + mu[0:1] * (prev - p) + mu[1:2] * (nxt - p)
    ps = jnp.where(valid, ps, 0.0)

    r = ps[:, 0:RWKV_WIDTH]
    k = ps[:, RWKV_WIDTH:2 * RWKV_WIDTH]
    v = ps[:, 2 * RWKV_WIDTH:3 * RWKV_WIDTH]
    lo = ps[:, 3 * RWKV_WIDTH:]
    lo_b = lo.astype(BF16)

    w = w0_ref[z] + _dot(jnp.tanh(lo).astype(BF16), w1_ref[z])
    lw = jnp.where(valid, -DECAY_SCALE * _sigmoid(w), 0.0)
    a_z = _sigmoid(a0_ref[z] + _dot(lo_b, w2_ref[z]))
    ones_b = ones_ref[...]
    kkv = k * kk_ref[...]
    n2 = _dot3_right(kkv * kkv, ones_b)
    kkn = kkv * jnp.minimum(lax.rsqrt(n2), 1e12)
    kdir = k * (1.0 + (a_z - 1.0) * ka_ref[...])

    sign = 1 - 2 * z
    ti = lax.broadcasted_iota(jnp.int32, (L, L), 0)
    si = lax.broadcasted_iota(jnp.int32, (L, L), 1)
    tri = jnp.where((ti - si) * sign >= 0, 1.0, 0.0).astype(BF16)
    c_inc = _dot3_left(tri, lw)
    ctot = jnp.where(z == 0, c_inc[L - 1:L, :], c_inc[0:1, :])
    e_neg = jnp.exp(-c_inc)
    g_rel = jnp.exp(ctot - c_inc)
    rh = r * jnp.exp(c_inc)
    ah = -kkn * jnp.exp(c_inc - lw)
    kka = kkn * a_z
    bh = kka * e_neg
    kh = kdir * e_neg
    bg = kka * g_rel
    kg = kdir * g_rel

    bkt_all = jnp.concatenate([bg, kg], axis=0).T
    gam_all = jnp.broadcast_to(jnp.exp(ctot), (LANES, RWKV_WIDTH)).T

    t2 = lax.broadcasted_iota(jnp.int32, (L, 2 * L), 0)
    s2 = lax.broadcasted_iota(jnp.int32, (L, 2 * L), 1) % L
    d2 = (t2 - s2) * sign
    strict2 = d2 > 0
    incl2 = d2 >= 0
    lane = lax.broadcasted_iota(jnp.int32, (1, LANES), 1)
    zeros_blk = jnp.zeros((L, LANES), F32)

    o_blocks = []
    for jp in range(RWKV_HEADS // 2):
        cols = slice(jp * LANES, (jp + 1) * LANES)
        ah_p, rh_p, v_p = ah[:, cols], rh[:, cols], v[:, cols]
        bk_p = jnp.concatenate([bh[:, cols], kh[:, cols]], axis=0).astype(BF16)
        o_pair = zeros_blk
        for e in range(2):
            h = 2 * jp + e
            mine = (lane >= HEAD_DIM) if e else (lane < HEAD_DIM)
            ah_e = jnp.where(mine, ah_p, 0.0)
            rh_e = jnp.where(mine, rh_p, 0.0)
            v_sw = pltpu.roll(jnp.where(mine, v_p, 0.0), HEAD_DIM, axis=1)
            g_a = jnp.where(strict2, _dot_nt(ah_e.astype(BF16), bk_p), 0.0)
            g_r = jnp.where(incl2, _dot_nt(rh_e.astype(BF16), bk_p), 0.0)
            v_sw_b = v_sw.astype(BF16)
            rhs0 = jnp.concatenate([jnp.zeros((L, LANES), BF16), v_sw_b], axis=0)
            x = ah_e + _dot(g_a.astype(BF16), rhs0)
            apow = g_a[:, :L]
            for it in range(6):
                ab = apow.astype(BF16)
                x = x + _dot(ab, x.astype(BF16))
                if it < 5:
                    apow = _dot(ab, ab)
            rhs1 = jnp.concatenate([x.astype(BF16), v_sw_b], axis=0)
            om = _dot(g_r.astype(BF16), rhs1)
            pq = _dot(bkt_all[h * HEAD_DIM:(h + 1) * HEAD_DIM, :].astype(BF16), rhs1)
            hst = hs_ref[h]
            hst_b = hst.astype(BF16)
            omega = rh_e + jnp.where(mine, om, 0.0)
            o_pair = o_pair + _dot(omega.astype(BF16), hst_b) + jnp.where(mine, 0.0, om)
            h_old = hst[e * HEAD_DIM:(e + 1) * HEAD_DIM, :]
            h_new = (gam_all[h * HEAD_DIM:(h + 1) * HEAD_DIM, :] * h_old
                     + _dot(jnp.where(mine, pq, 0.0).astype(BF16), hst_b)
                     + jnp.where(mine, 0.0, pq))
            zero_half = jnp.zeros((HEAD_DIM, LANES), F32)
            hs_ref[h] = jnp.concatenate([zero_half, h_new] if e else [h_new, zero_half], axis=0)
        o_blocks.append(pltpu.roll(o_pair, HEAD_DIM, axis=1))
    o = jnp.concatenate(o_blocks, axis=1)

    rows = pl.ds(pl.multiple_of(j * L, L), L)

    @pl.when(pz == 0)
    def _():
        ob_ref[rows, :] = o

    @pl.when(pz == 1)
    def _():
        osum = o + ob_ref[rows, :]
        inv_hd = 1.0 / HEAD_DIM
        mo = _dot3_right(osum, ones_b) * inv_hd
        dlt = osum - mo
        vo = _dot3_right(dlt * dlt, ones_b) * inv_hd
        on = dlt * lax.rsqrt(vo + GN_EPS) * gg_ref[...] + gb_ref[...]
        a_bwd = _sigmoid(a0_ref[1] + _dot(lo_b, w2_ref[1]))
        kdir_bwd = k * (1.0 + (a_bwd - 1.0) * ka_ref[...])
        bonus = _dot3_right(r * (kdir + kdir_bwd) * rk_ref[...], ones_b) * v
        gate = _dot(_sigmoid(lo).astype(BF16), w3_ref[...])
        out_ref[0] = (on + bonus) * gate


def _rwkv(pr, lp, ones512):
    B, Tp, _ = pr.shape
    n_chunks = Tp // CHUNK
    n8 = Tp // 8

    def jblk(pz, c):
        tau = jnp.where(pz == 1, c, n_chunks - 1 - c)
        return lax.rem(tau + n_chunks - 1, n_chunks)

    def const(shape):
        nd = len(shape)
        return pl.BlockSpec(shape, lambda bi, pz, c: (0,) * nd)

    in_specs = [
        pl.BlockSpec((1, CHUNK, RWKV_IN_WIDTH), lambda bi, pz, c: (bi, jblk(pz, c), 0)),
        pl.BlockSpec((1, 8, RWKV_IN_WIDTH),
                     lambda bi, pz, c: (bi, lax.rem(jblk(pz, c) * (CHUNK // 8) + n8 - 1, n8), 0)),
        pl.BlockSpec((1, 8, RWKV_IN_WIDTH),
                     lambda bi, pz, c: (bi, lax.rem((jblk(pz, c) + 1) * (CHUNK // 8), n8), 0)),
        const((2, RWKV_IN_WIDTH)),
        const((2, 1, RWKV_WIDTH)),
        const((2, LORA_WIDTH, RWKV_WIDTH)),
        const((2, 1, RWKV_WIDTH)),
        const((2, LORA_WIDTH, RWKV_WIDTH)),
        const((LORA_WIDTH, RWKV_WIDTH)),
        const((1, RWKV_WIDTH)),
        const((1, RWKV_WIDTH)),
        const((1, RWKV_WIDTH)),
        const((1, RWKV_WIDTH)),
        const((1, RWKV_WIDTH)),
        const((RWKV_WIDTH, RWKV_WIDTH)),
    ]
    out_spec = pl.BlockSpec(
        (1, CHUNK, RWKV_WIDTH),
        lambda bi, pz, c: (bi, jnp.where(pz == 1, jblk(pz, c), n_chunks - 1), 0))
    return pl.pallas_call(
        functools.partial(_rwkv_kernel, n_chunks=n_chunks),
        grid=(B, 2, n_chunks),
        in_specs=in_specs,
        out_specs=out_spec,
        out_shape=jax.ShapeDtypeStruct((B, Tp, RWKV_WIDTH), F32),
        scratch_shapes=[
            pltpu.VMEM((RWKV_HEADS, LANES, LANES), F32),
            pltpu.VMEM((Tp, RWKV_WIDTH), F32),
        ],
        compiler_params=pltpu.CompilerParams(
            dimension_semantics=("arbitrary", "arbitrary", "arbitrary"),
            vmem_limit_bytes=VMEM_LIMIT),
        name="rwkv",
    )(pr, pr, pr, lp["mu"], lp["w0"], lp["w1"], lp["a0"], lp["w2"], lp["w3"],
      lp["k_k"], lp["k_a"], lp["r_k"], lp["gn_g"], lp["gn_b"], ones512)


def _rms_rope(x, gain, cos, sin, ones_b):
    ms = _dot3_right(x * x, ones_b) * (1.0 / HEAD_DIM)
    xn = x * lax.rsqrt(ms + QK_EPS) * gain
    lane = lax.broadcasted_iota(jnp.int32, (1, LANES), 1)
    first = (lane % (2 * ROPE_FREQS)) < ROPE_FREQS
    partner = jnp.where(first, pltpu.roll(xn, LANES - ROPE_FREQS, axis=1),
                        pltpu.roll(xn, ROPE_FREQS, axis=1))
    return xn * cos + partner * sin


def _attn_kernel(q_ref, kv_ref, cq_ref, sq_ref, ck_ref, sk_ref, qg_ref, kg_ref, ones_ref,
                 o_ref, kt_ref, vv_ref, *, n_real, tq):
    N = n_real
    i = pl.program_id(1)
    ones_b = ones_ref[...]
    lane = lax.broadcasted_iota(jnp.int32, (1, LANES), 1)
    low = lane < HEAD_DIM

    @pl.when(i == 0)
    def _():
        def put(kn, vv, col0, width):
            kt = kn.T
            zero = jnp.zeros((HEAD_DIM, width), F32)
            cols = pl.ds(col0, width)
            kt_ref[0, :, cols] = jnp.concatenate([kt[:HEAD_DIM], zero], axis=0).astype(BF16)
            kt_ref[1, :, cols] = jnp.concatenate([zero, kt[:HEAD_DIM]], axis=0).astype(BF16)
            kt_ref[2, :, cols] = jnp.concatenate([kt[HEAD_DIM:], zero], axis=0).astype(BF16)
            kt_ref[3, :, cols] = jnp.concatenate([zero, kt[HEAD_DIM:]], axis=0).astype(BF16)
            vv_ref[0, cols, :] = vv.astype(BF16)
            vv_ref[1, cols, :] = pltpu.roll(vv, HEAD_DIM, axis=1).astype(BF16)

        ck = min(K_PREP_ROWS, N)

        def body(rb, carry):
            r0 = pl.multiple_of(rb * ck, ck)
            rows = pl.ds(r0, ck)
            kvc = kv_ref[0, rows, :]
            kn = _rms_rope(kvc[:, :KV_WIDTH], kg_ref[...], ck_ref[rows, :], sk_ref[rows, :], ones_b)
            put(kn, kvc[:, KV_WIDTH:], r0, ck)
            return carry

        lax.fori_loop(0, N // ck, body, 0)
        kvt = kv_ref[0, N:N + TAIL, :]
        knt = _rms_rope(kvt[:, :KV_WIDTH], kg_ref[...], ck_ref[N:N + TAIL, :], sk_ref[N:N + TAIL, :], ones_b)
        zpad = jnp.zeros((LANES - TAIL, LANES), F32)
        put(jnp.concatenate([knt, zpad], axis=0), jnp.concatenate([kvt[:, KV_WIDTH:], zpad], axis=0), N, LANES)

    tail_col = lax.broadcasted_iota(jnp.int32, (1, LANES), 1)
    tail_bias = jnp.where(jnp.logical_and(tail_col >= PAD, tail_col < TAIL), 0.0, NEG_BIG)

    q = q_ref[0]
    cq = cq_ref[...]
    sq = sq_ref[...]
    for jp in range(ATTN_WIDTH // LANES):
        qp = _rms_rope(q[:, jp * LANES:(jp + 1) * LANES], qg_ref[...], cq, sq, ones_b)
        qp = (qp * (HEAD_DIM ** -0.5)).astype(BF16)
        halves = []
        for e in range(2):
            g = (2 * jp + e) // 4
            kidx = 2 * g + e
            vidx = 0 if g == e else 1
            s_main = _dot(qp, kt_ref[kidx, :, :N])
            s_tail = _dot(qp, kt_ref[kidx, :, N:]) + tail_bias
            m = jnp.maximum(jnp.max(s_main, axis=-1, keepdims=True),
                            jnp.max(s_tail, axis=-1, keepdims=True))
            p_main = jnp.exp(s_main - m)
            p_tail = jnp.exp(s_tail - m)
            den = jnp.sum(p_main, axis=-1, keepdims=True) + jnp.sum(p_tail, axis=-1, keepdims=True)
            acc = (_dot(p_main.astype(BF16), vv_ref[vidx, :N, :])
                   + _dot(p_tail.astype(BF16), vv_ref[vidx, N:, :]))
            halves.append(acc / den)
        o_ref[0, :, jp * LANES:(jp + 1) * LANES] = jnp.where(low, halves[0], halves[1])


def _attention(q, kv, cos_t, sin_t, q_gain, k_gain, ones128):
    B, Tp, _ = q.shape
    N = Tp - TAIL
    tq = min(Q_TILE, N)
    nq = N // tq
    nk = N + LANES
    return pl.pallas_call(
        functools.partial(_attn_kernel, n_real=N, tq=tq),
        grid=(B, nq + 1),
        in_specs=[
            pl.BlockSpec((1, tq, ATTN_WIDTH), lambda bi, i: (bi, i, 0)),
            pl.BlockSpec((1, Tp, 2 * KV_WIDTH), lambda bi, i: (bi, 0, 0)),
            pl.BlockSpec((tq, LANES), lambda bi, i: (i, 0)),
            pl.BlockSpec((tq, LANES), lambda bi, i: (i, 0)),
            pl.BlockSpec((Tp, LANES), lambda bi, i: (0, 0)),
            pl.BlockSpec((Tp, LANES), lambda bi, i: (0, 0)),
            pl.BlockSpec((1, LANES), lambda bi, i: (0, 0)),
            pl.BlockSpec((1, LANES), lambda bi, i: (0, 0)),
            pl.BlockSpec((LANES, LANES), lambda bi, i: (0, 0)),
        ],
        out_specs=pl.BlockSpec((1, tq, ATTN_WIDTH), lambda bi, i: (bi, i, 0)),
        out_shape=jax.ShapeDtypeStruct((B, Tp, ATTN_WIDTH), F32),
        scratch_shapes=[
            pltpu.VMEM((4, LANES, nk), BF16),
            pltpu.VMEM((2, nk, LANES), BF16),
        ],
        compiler_params=pltpu.CompilerParams(
            dimension_semantics=("arbitrary", "arbitrary"), vmem_limit_bytes=VMEM_LIMIT),
        name="attention",
    )(q, kv, cos_t, sin_t, cos_t, sin_t, q_gain, k_gain, ones128)


def _post_kernel(h_ref, yr_ref, ya_ref, wor_ref, woa_ref, g1_ref, b1_ref,
                 wg_ref, wu_ref, wd_ref, g2_ref, b2_ref, o_ref):
    mix = _dot(yr_ref[...].astype(BF16), wor_ref[...]) + _dot(ya_ref[...].astype(BF16), woa_ref[...])
    x1 = _layer_norm(DEEPNORM_ALPHA * h_ref[...] + mix, g1_ref[...], b1_ref[...])
    x1b = x1.astype(BF16)
    ffn = None
    for jc in range(D_FF // FF_CHUNK):
        cols = slice(jc * FF_CHUNK, (jc + 1) * FF_CHUNK)
        gate = _dot(x1b, wg_ref[:, cols])
        up = _dot(x1b, wu_ref[:, cols])
        act = (gate * _sigmoid(gate) * up).astype(BF16)
        part = _dot(act, wd_ref[cols, :])
        ffn = part if ffn is None else ffn + part
    o_ref[...] = _layer_norm(DEEPNORM_ALPHA * x1 + ffn, g2_ref[...], b2_ref[...])


def _post(h2, yr2, ya2, lp):
    M, D = h2.shape
    tm = _pick_tile(M, ROW_TILE)

    def const(shape):
        return pl.BlockSpec(shape, lambda i: (0, 0), pipeline_mode=pl.Buffered(1))

    return pl.pallas_call(
        _post_kernel,
        grid=(M // tm,),
        in_specs=[
            pl.BlockSpec((tm, D), lambda i: (i, 0)),
            pl.BlockSpec((tm, RWKV_WIDTH), lambda i: (i, 0)),
            pl.BlockSpec((tm, ATTN_WIDTH), lambda i: (i, 0)),
            const((RWKV_WIDTH, D)),
            const((ATTN_WIDTH, D)),
            const((1, D)),
            const((1, D)),
            const((D, D_FF)),
            const((D, D_FF)),
            const((D_FF, D)),
            const((1, D)),
            const((1, D)),
        ],
        out_specs=pl.BlockSpec((tm, D), lambda i: (i, 0)),
        out_shape=jax.ShapeDtypeStruct((M, D), F32),
        compiler_params=pltpu.CompilerParams(
            dimension_semantics=("arbitrary",), vmem_limit_bytes=VMEM_LIMIT),
        name="post",
    )(h2, yr2, ya2, lp["wo_r"], lp["wo_a"], lp["ln1_g"], lp["ln1_b"],
      lp["w_gate"], lp["w_up"], lp["w_down"], lp["ln2_g"], lp["ln2_b"])


def _rope_tables(n_real):
    tok = jnp.arange(n_real)
    pos = jnp.stack([tok // GRID_W, tok % GRID_W], axis=-1).astype(F32)
    inv_freq = ROPE_THETA ** (-jnp.arange(ROPE_FREQS, dtype=F32) / ROPE_FREQS)
    ang = pos[:, :, None] * inv_freq
    ang = jnp.concatenate([ang, jnp.zeros((TAIL, 2, ROPE_FREQS), F32)], axis=0)
    cos, sin = jnp.cos(ang), jnp.sin(ang)
    cos64 = jnp.concatenate([cos[:, 0], cos[:, 0], cos[:, 1], cos[:, 1]], axis=-1)
    sin64 = jnp.concatenate([-sin[:, 0], sin[:, 0], -sin[:, 1], sin[:, 1]], axis=-1)
    return jnp.tile(cos64, (1, 2)), jnp.tile(sin64, (1, 2))


def _block_ones(width):
    idx = jnp.arange(width) // HEAD_DIM
    return (idx[:, None] == idx[None, :]).astype(BF16)


def _layer_params(l, w_in, shift_mu, decay_w0, decay_up, iclr_a0, iclr_up, gate_up, k_k, k_a, r_k,
                  gn_g, gn_b, q_gain, k_gain, w_out, ln1_g, ln1_b, w_ffn_in, w_ffn_out, ln2_g, ln2_b):
    def lora(up, row0):
        rows = up.shape[-2]
        full = jnp.zeros(up.shape[:-2] + (LORA_WIDTH, RWKV_WIDTH), F32)
        return full.at[..., row0:row0 + rows, :].set(up).astype(BF16)

    row = lambda a: a.reshape(1, -1)
    return dict(
        w_in=w_in[l].astype(BF16),
        mu=shift_mu[l],
        w0=decay_w0[l].reshape(2, 1, RWKV_WIDTH),
        w1=lora(decay_up[l], 0),
        a0=iclr_a0[l].reshape(2, 1, RWKV_WIDTH),
        w2=lora(iclr_up[l], 32),
        w3=lora(gate_up[l], 64),
        k_k=row(k_k[l]), k_a=row(k_a[l]), r_k=row(r_k[l]), gn_g=row(gn_g[l]), gn_b=row(gn_b[l]),
        q_gain=jnp.tile(row(q_gain[l]), (1, 2)), k_gain=jnp.tile(row(k_gain[l]), (1, 2)),
        wo_r=w_out[l, :RWKV_WIDTH].astype(BF16), wo_a=w_out[l, RWKV_WIDTH:].astype(BF16),
        ln1_g=row(ln1_g[l]), ln1_b=row(ln1_b[l]),
        w_gate=w_ffn_in[l, :, :D_FF].astype(BF16), w_up=w_ffn_in[l, :, D_FF:].astype(BF16),
        w_down=w_ffn_out[l].astype(BF16),
        ln2_g=row(ln2_g[l]), ln2_b=row(ln2_b[l]),
    )


def _trunk(x, meta_tokens, ln_in_g, ln_in_b, layers, ones512, ones128):
    B, N, D = x.shape
    Tp = N + TAIL
    cos_t, sin_t = _rope_tables(N)
    h = _embed(x, meta_tokens, ln_in_g, ln_in_b)
    for lp in layers:
        pr, q, kv = _proj(h.reshape(B * Tp, D), lp["w_in"])
        y_rwkv = _rwkv(pr.reshape(B, Tp, RWKV_IN_WIDTH), lp, ones512)
        y_attn = _attention(q.reshape(B, Tp, ATTN_WIDTH), kv.reshape(B, Tp, 2 * KV_WIDTH),
                            cos_t, sin_t, lp["q_gain"], lp["k_gain"], ones128)
        h = _post(h.reshape(B * Tp, D), y_rwkv.reshape(B * Tp, RWKV_WIDTH),
                  y_attn.reshape(B * Tp, ATTN_WIDTH), lp).reshape(B, Tp, D)
    return h[:, :N]


def kernel(x_prompt, x_sample, meta_tokens, ln_in_g, ln_in_b, w_in, shift_mu, decay_w0, decay_up,
           iclr_a0, iclr_up, gate_up, k_k, k_a, r_k, gn_g, gn_b, q_gain, k_gain, w_out,
           ln1_g, ln1_b, w_ffn_in, w_ffn_out, ln2_g, ln2_b):
    layers = [
        _layer_params(l, w_in, shift_mu, decay_w0, decay_up, iclr_a0, iclr_up, gate_up, k_k, k_a, r_k,
                      gn_g, gn_b, q_gain, k_gain, w_out, ln1_g, ln1_b, w_ffn_in, w_ffn_out, ln2_g, ln2_b)
        for l in range(w_in.shape[0])
    ]
    ones512 = _block_ones(RWKV_WIDTH)
    ones128 = _block_ones(LANES)
    y_prompt = _trunk(x_prompt, meta_tokens, ln_in_g, ln_in_b, layers, ones512, ones128)
    y_sample = _trunk(x_sample, meta_tokens, ln_in_g, ln_in_b, layers, ones512, ones128)
    return (y_prompt, y_sample)
```

```python
import functools
import math

import jax
import jax.numpy as jnp
from jax import lax
from jax.experimental import pallas as pl
from jax.experimental.pallas import tpu as pltpu

F32 = jnp.float32
BF16 = jnp.bfloat16

D_MODEL = 1024
DEPTH = 4
HEAD_DIM = 64
RWKV_WIDTH = 512
ATTN_WIDTH = 512
KV_WIDTH = 128
LORA_WIDTH = 128
RWKV_IN_WIDTH = 3 * RWKV_WIDTH + LORA_WIDTH
IN_WIDTH = RWKV_IN_WIDTH + ATTN_WIDTH + 2 * KV_WIDTH
D_FF = 2816
FF_CHUNK = 1408
N_META = 16
GRID_W = 64
ROPE_THETA = 10000.0
ROPE_FREQS = 16
DEEPNORM_ALPHA = (2.0 * DEPTH) ** 0.25
LN_EPS = 1e-5
GN_EPS = 64e-5
QK_EPS = 1e-6
DECAY_SCALE = math.exp(-0.5)

LANES = 128
MXU_TILE = 256
CHUNK = 64
SUPER = 2 * CHUNK
GROUP = 4 * HEAD_DIM
HEADS_PER_GROUP = GROUP // HEAD_DIM
TAIL = 128
PAD = TAIL - N_META
ROW_TILE = 512
Q_TILE = 256
K_PREP_ROWS = 512
NEG_BIG = -1e30
VMEM_LIMIT = 56 * 1024 * 1024

assert GROUP == MXU_TILE and RWKV_WIDTH == 2 * GROUP and SUPER == LANES


def _dot(a, b):
    return jnp.dot(a, b, preferred_element_type=F32)


def _dot_nt(a, b):
    return lax.dot_general(a, b, (((1,), (1,)), ((), ())), preferred_element_type=F32)


def _split3(x):
    hi = x.astype(BF16)
    r1 = x - hi.astype(F32)
    mid = r1.astype(BF16)
    lo = (r1 - mid.astype(F32)).astype(BF16)
    return hi, mid, lo


def _dot3_right(x, m):
    hi, mid, lo = _split3(x)
    return _dot(hi, m) + _dot(mid, m) + _dot(lo, m)


def _dot3_left(m, x):
    hi, mid, lo = _split3(x)
    return _dot(m, hi) + _dot(m, mid) + _dot(m, lo)


def _layer_norm(x, g, b):
    mu = jnp.mean(x, axis=-1, keepdims=True)
    xc = x - mu
    var = jnp.mean(xc * xc, axis=-1, keepdims=True)
    return xc * lax.rsqrt(var + LN_EPS) * g + b


def _sigmoid(x):
    return 1.0 / (1.0 + jnp.exp(-x))


def _pick_tile(total, pref):
    t = pref
    while total % t:
        t //= 2
    return t


def _embed_kernel(x_ref, meta_ref, g_ref, b_ref, o_ref, *, n_x_tiles):
    i = pl.program_id(1)

    @pl.when(i < n_x_tiles)
    def _():
        o_ref[0] = _layer_norm(x_ref[0], g_ref[...], b_ref[...])

    @pl.when(i == n_x_tiles)
    def _():
        o_ref[0] = jnp.zeros(o_ref.shape[1:], F32)
        o_ref[0, PAD:TAIL, :] = _layer_norm(meta_ref[...], g_ref[...], b_ref[...])


def _embed(x, meta, g, b):
    B, N, D = x.shape
    te = _pick_tile(N, ROW_TILE)
    n_x_tiles = N // te
    assert te >= TAIL
    return pl.pallas_call(
        functools.partial(_embed_kernel, n_x_tiles=n_x_tiles),
        grid=(B, n_x_tiles + 1),
        in_specs=[
            pl.BlockSpec((1, te, D), lambda bi, i: (bi, jnp.minimum(i, n_x_tiles - 1), 0)),
            pl.BlockSpec((N_META, D), lambda bi, i: (0, 0)),
            pl.BlockSpec((1, D), lambda bi, i: (0, 0)),
            pl.BlockSpec((1, D), lambda bi, i: (0, 0)),
        ],
        out_specs=pl.BlockSpec((1, te, D), lambda bi, i: (bi, i, 0)),
        out_shape=jax.ShapeDtypeStruct((B, N + TAIL, D), F32),
        compiler_params=pltpu.CompilerParams(dimension_semantics=("arbitrary", "arbitrary")),
        name="embed",
    )(x, meta, g.reshape(1, D), b.reshape(1, D))


def _proj_kernel(h_ref, w_ref, pr_ref, q_ref, kv_ref):
    y = _dot(h_ref[...].astype(BF16), w_ref[...])
    pr_ref[...] = y[:, :RWKV_IN_WIDTH]
    q_ref[...] = y[:, RWKV_IN_WIDTH:RWKV_IN_WIDTH + ATTN_WIDTH]
    kv_ref[...] = y[:, RWKV_IN_WIDTH + ATTN_WIDTH:]


def _proj(h2, w_in_b):
    M, D = h2.shape
    tm = _pick_tile(M, ROW_TILE)
    return pl.pallas_call(
        _proj_kernel,
        grid=(M // tm,),
        in_specs=[
            pl.BlockSpec((tm, D), lambda i: (i, 0)),
            pl.BlockSpec((D, IN_WIDTH), lambda i: (0, 0)),
        ],
        out_specs=[
            pl.BlockSpec((tm, RWKV_IN_WIDTH), lambda i: (i, 0)),
            pl.BlockSpec((tm, ATTN_WIDTH), lambda i: (i, 0)),
            pl.BlockSpec((tm, 2 * KV_WIDTH), lambda i: (i, 0)),
        ],
        out_shape=[
            jax.ShapeDtypeStruct((M, RWKV_IN_WIDTH), F32),
            jax.ShapeDtypeStruct((M, ATTN_WIDTH), F32),
            jax.ShapeDtypeStruct((M, 2 * KV_WIDTH), F32),
        ],
        compiler_params=pltpu.CompilerParams(
            dimension_semantics=("arbitrary",), vmem_limit_bytes=VMEM_LIMIT),
        name="proj",
    )(h2, w_in_b)


def _rwkv_kernel(*refs, direction, n_super):
    z = direction
    if z == 0:
        (pr_ref, prev_ref, next_ref, ob_ref, mu_ref, w0_ref, w1_ref, a0_ref, w2_ref, a0o_ref, w2o_ref,
         w3_ref, kk_ref, ka_ref, rk_ref, gg_ref, gb_ref, ones_ref, out_ref, hs_ref) = refs
    else:
        (pr_ref, prev_ref, next_ref, mu_ref, w0_ref, w1_ref, a0_ref, w2_ref,
         kk_ref, ka_ref, ones_ref, out_ref, hs_ref) = refs
    L, R = CHUNK, SUPER
    sign = 1 - 2 * z
    c = pl.program_id(1)
    cs = c if z == 0 else n_super - 1 - c
    sj = lax.rem(cs + n_super - 1, n_super)

    @pl.when(c == 0)
    def _():
        hs_ref[...] = jnp.zeros(hs_ref.shape, F32)

    row = lax.broadcasted_iota(jnp.int32, (R, 1), 0)
    valid = jnp.logical_or(sj != n_super - 1, row >= PAD)

    p = jnp.where(valid, pr_ref[0], 0.0)
    p_before = prev_ref[0, 7:8, :]
    p_after = jnp.where(sj == n_super - 2, 0.0, next_ref[0, 0:1, :])
    prev = jnp.where(row == 0, p_before, pltpu.roll(p, 1, axis=0))
    nxt = jnp.where(row == R - 1, p_after, pltpu.roll(p, R - 1, axis=0))
    mu = mu_ref[...]
    ps = p + mu[0:1] * (prev - p) + mu[1:2] * (nxt - p)
    ps = jnp.where(valid, ps, 0.0)

    r = ps[:, 0:RWKV_WIDTH]
    k = ps[:, RWKV_WIDTH:2 * RWKV_WIDTH]
    v = ps[:, 2 * RWKV_WIDTH:3 * RWKV_WIDTH]
    lo = ps[:, 3 * RWKV_WIDTH:]
    lo_b = lo.astype(BF16)
    ones_b = ones_ref[...]

    def segsum(x):
        return jnp.concatenate([_dot3_right(x[:, :GROUP], ones_b), _dot3_right(x[:, GROUP:], ones_b)], axis=1)

    w = w0_ref[...] + _dot(jnp.tanh(lo).astype(BF16), w1_ref[...])
    lw = jnp.where(valid, -DECAY_SCALE * _sigmoid(w), 0.0)
    a_z = _sigmoid(a0_ref[...] + _dot(lo_b, w2_ref[...]))
    kkv = k * kk_ref[...]
    kkn = kkv * jnp.minimum(lax.rsqrt(segsum(kkv * kkv)), 1e12)
    kdir = k * (1.0 + (a_z - 1.0) * ka_ref[...])

    ti = lax.broadcasted_iota(jnp.int32, (R, R), 0)
    si = lax.broadcasted_iota(jnp.int32, (R, R), 1)
    tri = jnp.where(jnp.logical_and(ti // L == si // L, (ti - si) * sign >= 0), 1.0, 0.0).astype(BF16)
    c_inc = _dot3_left(tri, lw)
    last = L - 1 if z == 0 else 0
    ctot = [c_inc[ch * L + last:ch * L + last + 1, :] for ch in range(2)]
    ctot_rows = jnp.concatenate([jnp.broadcast_to(ct, (L, RWKV_WIDTH)) for ct in ctot], axis=0)
    e_neg = jnp.exp(-c_inc)
    g_rel = jnp.exp(ctot_rows - c_inc)
    rh = r * jnp.exp(c_inc)
    ah = -kkn * jnp.exp(c_inc - lw)
    kka = kkn * a_z
    bh = kka * e_neg
    kh = kdir * e_neg
    bg = kka * g_rel
    kg = kdir * g_rel

    bkt = [jnp.concatenate([bg[ch * L:(ch + 1) * L], kg[ch * L:(ch + 1) * L]], axis=0).T for ch in range(2)]
    gam = [jnp.broadcast_to(jnp.exp(ct), (LANES, RWKV_WIDTH)).T for ct in ctot]

    lane_g = lax.broadcasted_iota(jnp.int32, (1, GROUP), 1)
    hm = [lane_g // HEAD_DIM == i for i in range(HEADS_PER_GROUP)]
    d4 = (lax.broadcasted_iota(jnp.int32, (L, GROUP), 0)
          - lax.broadcasted_iota(jnp.int32, (L, GROUP), 1) % L) * sign
    strict4 = d4 > 0
    incl4 = d4 >= 0
    eye4 = jnp.where(d4 == 0, 1.0, 0.0)
    bdmask = (lax.broadcasted_iota(jnp.int32, (GROUP, GROUP), 0) // HEAD_DIM
              == lax.broadcasted_iota(jnp.int32, (GROUP, GROUP), 1) // HEAD_DIM)
    zb = jnp.zeros((), BF16)

    def blocks(x4b):
        return [jnp.where(hm[i], x4b, zb) for i in range(HEADS_PER_GROUP)]

    def bd(x4b):
        return jnp.concatenate(blocks(x4b), axis=0)

    probs = [(ch, gq) for ch in range(2) for gq in range(2)]

    def sl(pb):
        ch, gq = pb
        return slice(ch * L, (ch + 1) * L), slice(gq * GROUP, (gq + 1) * GROUP)

    aab, aak, arb, ark = {}, {}, {}, {}
    for pb in probs:
        rs, ls = sl(pb)
        lhs = jnp.concatenate([ah[rs, ls], rh[rs, ls]], axis=0).astype(BF16)
        rhs = jnp.concatenate(blocks(bh[rs, ls].astype(BF16)) + blocks(kh[rs, ls].astype(BF16)), axis=0)
        g = _dot_nt(lhs, rhs)
        aab[pb] = jnp.where(strict4, g[:L, :GROUP], 0.0)
        aak[pb] = jnp.where(strict4, g[:L, GROUP:], 0.0)
        arb[pb] = jnp.where(incl4, g[L:, :GROUP], 0.0)
        ark[pb] = jnp.where(incl4, g[L:, GROUP:], 0.0)

    vblk, wv, pw, tinv = {}, {}, {}, {}
    for pb in probs:
        rs, ls = sl(pb)
        vblk[pb] = blocks(v[rs, ls].astype(BF16))
        aab_b = aab[pb].astype(BF16)
        wv[pb] = _dot(aak[pb].astype(BF16), jnp.concatenate(vblk[pb], axis=0))
        pw[pb] = _dot(aab_b, bd(aab_b))
        tinv[pb] = eye4 + aab[pb]

    for _ in range(4):
        for pb in probs:
            p_b = pw[pb].astype(BF16)
            res = _dot(jnp.concatenate([tinv[pb].astype(BF16), p_b], axis=0), bd(p_b))
            tinv[pb] = tinv[pb] + res[:L]
            pw[pb] = res[L:]
    for pb in probs:
        tinv[pb] = tinv[pb] + _dot(tinv[pb].astype(BF16), bd(pw[pb].astype(BF16)))

    t_b, u_b = {}, {}
    for pb in probs:
        rs, ls = sl(pb)
        rhs = jnp.concatenate([bd(ah[rs, ls].astype(BF16)), bd(wv[pb].astype(BF16))], axis=1)
        tu = _dot(tinv[pb].astype(BF16), rhs)
        t_b[pb] = tu[:, :GROUP].astype(BF16)
        u_b[pb] = tu[:, GROUP:].astype(BF16)

    omega, oloc, pc, qq = {}, {}, {}, {}
    zero_blk = jnp.zeros((L, GROUP), BF16)
    for pb in probs:
        ch, gq = pb
        rs, ls = sl(pb)
        tblk, ublk = blocks(t_b[pb]), blocks(u_b[pb])
        omega[pb] = rh[rs, ls] + _dot(arb[pb].astype(BF16), jnp.concatenate(tblk, axis=0))
        oloc[pb] = _dot(jnp.concatenate([arb[pb], ark[pb]], axis=1).astype(BF16),
                        jnp.concatenate(ublk + vblk[pb], axis=0))
        bkt4 = jnp.concatenate(
            [bkt[ch][gq * GROUP + i * HEAD_DIM:gq * GROUP + (i + 1) * HEAD_DIM, :]
             for i in range(HEADS_PER_GROUP)], axis=1).astype(BF16)
        rhs_p, rhs_q = [], []
        for i in range(HEADS_PER_GROUP):
            rhs_p += [tblk[i], zero_blk]
            rhs_q += [ublk[i], vblk[pb][i]]
        pq = _dot(bkt4, jnp.concatenate([jnp.concatenate(rhs_p, axis=0), jnp.concatenate(rhs_q, axis=0)], axis=1))
        pc[pb] = pq[:, :GROUP]
        qq[pb] = pq[:, GROUP:]

    o4 = {}
    hbd = [hs_ref[gq] for gq in range(2)]
    for ch in ((0, 1) if z == 0 else (1, 0)):
        for gq in range(2):
            pb = (ch, gq)
            hb = hbd[gq].astype(BF16)
            o4[pb] = _dot(omega[pb].astype(BF16), hb) + oloc[pb]
            upd = _dot(pc[pb].astype(BF16), hb) + qq[pb]
            g_rows = gam[ch][gq * GROUP:(gq + 1) * GROUP, :]
            hbd[gq] = (jnp.concatenate([g_rows, g_rows], axis=1) * hbd[gq]
                       + jnp.where(bdmask, jnp.concatenate([upd] * HEADS_PER_GROUP, axis=0), 0.0))
    for gq in range(2):
        hs_ref[gq] = hbd[gq]
    o = jnp.concatenate([jnp.concatenate([o4[(ch, 0)], o4[(ch, 1)]], axis=1) for ch in range(2)], axis=0)

    if z == 1:
        out_ref[0] = o
    else:
        osum = o + ob_ref[0]
        inv_hd = 1.0 / HEAD_DIM
        mo = segsum(osum) * inv_hd
        dlt = osum - mo
        vo = segsum(dlt * dlt) * inv_hd
        on = dlt * lax.rsqrt(vo + GN_EPS) * gg_ref[...] + gb_ref[...]
        a_bwd = _sigmoid(a0o_ref[...] + _dot(lo_b, w2o_ref[...]))
        kdir_bwd = k * (1.0 + (a_bwd - 1.0) * ka_ref[...])
        bonus = segsum(r * (kdir + kdir_bwd) * rk_ref[...]) * v
        gate = _dot(_sigmoid(lo).astype(BF16), w3_ref[...])
        out_ref[0] = (on + bonus) * gate


def _rwkv_dir(pr, lp, ones256, direction, o_bwd=None):
    B, Tp, _ = pr.shape
    n_super = Tp // SUPER
    n8 = Tp // 8
    z = direction

    def sblk(c):
        cs = c if z == 0 else n_super - 1 - c
        return lax.rem(cs + n_super - 1, n_super)

    def const(shape):
        return pl.BlockSpec(shape, lambda bi, c: (0,) * len(shape))

    row_spec = lambda width: pl.BlockSpec((1, SUPER, width), lambda bi, c: (bi, sblk(c), 0))
    vec = const((1, RWKV_WIDTH))
    lora = const((LORA_WIDTH, RWKV_WIDTH))
    in_specs = [
        row_spec(RWKV_IN_WIDTH),
        pl.BlockSpec((1, 8, RWKV_IN_WIDTH),
                     lambda bi, c: (bi, lax.rem(sblk(c) * (SUPER // 8) + n8 - 1, n8), 0)),
        pl.BlockSpec((1, 8, RWKV_IN_WIDTH),
                     lambda bi, c: (bi, lax.rem((sblk(c) + 1) * (SUPER // 8), n8), 0)),
    ]
    args = [pr, pr, pr]
    if z == 0:
        in_specs.append(row_spec(RWKV_WIDTH))
        args.append(o_bwd)
    in_specs += [const((2, RWKV_IN_WIDTH)), vec, lora, vec, lora]
    args += [lp["mu"], lp["w0"][z], lp["w1"][z], lp["a0"][z], lp["w2"][z]]
    if z == 0:
        in_specs += [vec, lora, lora]
        args += [lp["a0"][1], lp["w2"][1], lp["w3"]]
    in_specs += [vec, vec]
    args += [lp["k_k"], lp["k_a"]]
    if z == 0:
        in_specs += [vec, vec, vec]
        args += [lp["r_k"], lp["gn_g"], lp["gn_b"]]
    in_specs.append(const((GROUP, GROUP)))
    args.append(ones256)
    return pl.pallas_call(
        functools.partial(_rwkv_kernel, direction=z, n_super=n_super),
        grid=(B, n_super),
        in_specs=in_specs,
        out_specs=row_spec(RWKV_WIDTH),
        out_shape=jax.ShapeDtypeStruct((B, Tp, RWKV_WIDTH), F32),
        scratch_shapes=[pltpu.VMEM((2, GROUP, GROUP), F32)],
        compiler_params=pltpu.CompilerParams(
            dimension_semantics=("arbitrary", "arbitrary"), vmem_limit_bytes=VMEM_LIMIT),
        name="rwkv_fwd" if z == 0 else "rwkv_bwd",
    )(*args)


def _rms_rope(x, gain, cos, sin, ones_b):
    ms = _dot3_right(x * x, ones_b) * (1.0 / HEAD_DIM)
    xn = x * lax.rsqrt(ms + QK_EPS) * gain
    lane = lax.broadcasted_iota(jnp.int32, (1, LANES), 1)
    first = (lane % (2 * ROPE_FREQS)) < ROPE_FREQS
    partner = jnp.where(first, pltpu.roll(xn, LANES - ROPE_FREQS, axis=1),
                        pltpu.roll(xn, ROPE_FREQS, axis=1))
    return xn * cos + partner * sin


def _attn_kernel(q_ref, kv_ref, cq_ref, sq_ref, ck_ref, sk_ref, qg_ref, kg_ref, ones_ref,
                 o_ref, kt_ref, vv_ref, *, n_real):
    N = n_real
    i = pl.program_id(1)
    ones_b = ones_ref[...]
    lane = lax.broadcasted_iota(jnp.int32, (1, LANES), 1)
    low = lane < HEAD_DIM

    @pl.when(i == 0)
    def _():
        def put(kn, vv, col0, width):
            kt = kn.T
            zero = jnp.zeros((HEAD_DIM, width), F32)
            cols = pl.ds(col0, width)
            kt_ref[0, :, cols] = jnp.concatenate([kt[:HEAD_DIM], zero], axis=0).astype(BF16)
            kt_ref[1, :, cols] = jnp.concatenate([zero, kt[:HEAD_DIM]], axis=0).astype(BF16)
            kt_ref[2, :, cols] = jnp.concatenate([kt[HEAD_DIM:], zero], axis=0).astype(BF16)
            kt_ref[3, :, cols] = jnp.concatenate([zero, kt[HEAD_DIM:]], axis=0).astype(BF16)
            vv_ref[0, cols, :] = vv.astype(BF16)
            vv_ref[1, cols, :] = pltpu.roll(vv, HEAD_DIM, axis=1).astype(BF16)

        ck = min(K_PREP_ROWS, N)

        def body(rb, carry):
            r0 = pl.multiple_of(rb * ck, ck)
            rows = pl.ds(r0, ck)
            kvc = kv_ref[0, rows, :]
            kn = _rms_rope(kvc[:, :KV_WIDTH], kg_ref[...], ck_ref[rows, :], sk_ref[rows, :], ones_b)
            put(kn, kvc[:, KV_WIDTH:], r0, ck)
            return carry

        lax.fori_loop(0, N // ck, body, 0)
        kvt = kv_ref[0, N:N + TAIL, :]
        knt = _rms_rope(kvt[:, :KV_WIDTH], kg_ref[...], ck_ref[N:N + TAIL, :], sk_ref[N:N + TAIL, :], ones_b)
        put(knt, kvt[:, KV_WIDTH:], N, TAIL)

    tail_col = lax.broadcasted_iota(jnp.int32, (1, TAIL), 1)
    tail_bias = jnp.where(tail_col >= PAD, 0.0, NEG_BIG)

    q = q_ref[0]
    cq = cq_ref[...]
    sq = sq_ref[...]
    heads = [(jp, e) for jp in range(ATTN_WIDTH // LANES) for e in range(2)]
    qp = [(_rms_rope(q[:, jp * LANES:(jp + 1) * LANES], qg_ref[...], cq, sq, ones_b)
           * (HEAD_DIM ** -0.5)).astype(BF16) for jp in range(ATTN_WIDTH // LANES)]

    def scores(jp, e):
        kidx = 2 * ((2 * jp + e) // 4) + e
        return _dot(qp[jp], kt_ref[kidx, :, :N]), _dot(qp[jp], kt_ref[kidx, :, N:]) + tail_bias

    halves = {}
    nxt = scores(*heads[0])
    for n, (jp, e) in enumerate(heads):
        s_main, s_tail = nxt
        if n + 1 < len(heads):
            nxt = scores(*heads[n + 1])
        vidx = 0 if (2 * jp + e) // 4 == e else 1
        m = jnp.maximum(jnp.max(s_main, axis=-1, keepdims=True), jnp.max(s_tail, axis=-1, keepdims=True))
        p_main = jnp.exp(s_main - m)
        p_tail = jnp.exp(s_tail - m)
        den = jnp.sum(p_main, axis=-1, keepdims=True) + jnp.sum(p_tail, axis=-1, keepdims=True)
        acc = (_dot(p_main.astype(BF16), vv_ref[vidx, :N, :])
               + _dot(p_tail.astype(BF16), vv_ref[vidx, N:, :]))
        halves[(jp, e)] = acc / den
        if e == 1:
            o_ref[0, :, jp * LANES:(jp + 1) * LANES] = jnp.where(low, halves[(jp, 0)], halves[(jp, 1)])


def _attention(q, kv, cos_t, sin_t, q_gain, k_gain, ones128):
    B, Tp, _ = q.shape
    N = Tp - TAIL
    tq = min(Q_TILE, N)
    nq = N // tq
    nk = N + TAIL
    return pl.pallas_call(
        functools.partial(_attn_kernel, n_real=N),
        grid=(B, nq + 1),
        in_specs=[
            pl.BlockSpec((1, tq, ATTN_WIDTH), lambda bi, i: (bi, i, 0)),
            pl.BlockSpec((1, Tp, 2 * KV_WIDTH), lambda bi, i: (bi, 0, 0)),
            pl.BlockSpec((tq, LANES), lambda bi, i: (i, 0)),
            pl.BlockSpec((tq, LANES), lambda bi, i: (i, 0)),
            pl.BlockSpec((Tp, LANES), lambda bi, i: (0, 0)),
            pl.BlockSpec((Tp, LANES), lambda bi, i: (0, 0)),
            pl.BlockSpec((1, LANES), lambda bi, i: (0, 0)),
            pl.BlockSpec((1, LANES), lambda bi, i: (0, 0)),
            pl.BlockSpec((LANES, LANES), lambda bi, i: (0, 0)),
        ],
        out_specs=pl.BlockSpec((1, tq, ATTN_WIDTH), lambda bi, i: (bi, i, 0)),
        out_shape=jax.ShapeDtypeStruct((B, Tp, ATTN_WIDTH), F32),
        scratch_shapes=[
            pltpu.VMEM((4, LANES, nk), BF16),
            pltpu.VMEM((2, nk, LANES), BF16),
        ],
        compiler_params=pltpu.CompilerParams(
            dimension_semantics=("arbitrary", "arbitrary"), vmem_limit_bytes=VMEM_LIMIT),
        name="attention",
    )(q, kv, cos_t, sin_t, cos_t, sin_t, q_gain, k_gain, ones128)


def _post_kernel(h_ref, yr_ref, ya_ref, wor_ref, woa_ref, g1_ref, b1_ref,
                 wg_ref, wu_ref, wd_ref, g2_ref, b2_ref, o_ref):
    mix = _dot(yr_ref[...].astype(BF16), wor_ref[...]) + _dot(ya_ref[...].astype(BF16), woa_ref[...])
    x1 = _layer_norm(DEEPNORM_ALPHA * h_ref[...] + mix, g1_ref[...], b1_ref[...])
    x1b = x1.astype(BF16)
    ffn = None
    for jc in range(D_FF // FF_CHUNK):
        cols = slice(jc * FF_CHUNK, (jc + 1) * FF_CHUNK)
        gate = _dot(x1b, wg_ref[:, cols])
        up = _dot(x1b, wu_ref[:, cols])
        act = (gate * _sigmoid(gate) * up).astype(BF16)
        part = _dot(act, wd_ref[cols, :])
        ffn = part if ffn is None else ffn + part
    o_ref[...] = _layer_norm(DEEPNORM_ALPHA * x1 + ffn, g2_ref[...], b2_ref[...])


def _post(h2, yr2, ya2, lp):
    M, D = h2.shape
    tm = _pick_tile(M, ROW_TILE)

    def const(shape):
        return pl.BlockSpec(shape, lambda i: (0, 0), pipeline_mode=pl.Buffered(1))

    return pl.pallas_call(
        _post_kernel,
        grid=(M // tm,),
        in_specs=[
            pl.BlockSpec((tm, D), lambda i: (i, 0)),
            pl.BlockSpec((tm, RWKV_WIDTH), lambda i: (i, 0)),
            pl.BlockSpec((tm, ATTN_WIDTH), lambda i: (i, 0)),
            const((RWKV_WIDTH, D)),
            const((ATTN_WIDTH, D)),
            const((1, D)),
            const((1, D)),
            const((D, D_FF)),
            const((D, D_FF)),
            const((D_FF, D)),
            const((1, D)),
            const((1, D)),
        ],
        out_specs=pl.BlockSpec((tm, D), lambda i: (i, 0)),
        out_shape=jax.ShapeDtypeStruct((M, D), F32),
        compiler_params=pltpu.CompilerParams(
            dimension_semantics=("arbitrary",), vmem_limit_bytes=VMEM_LIMIT),
        name="post",
    )(h2, yr2, ya2, lp["wo_r"], lp["wo_a"], lp["ln1_g"], lp["ln1_b"],
      lp["w_gate"], lp["w_up"], lp["w_down"], lp["ln2_g"], lp["ln2_b"])


def _rope_tables(n_real):
    tok = jnp.arange(n_real)
    pos = jnp.stack([tok // GRID_W, tok % GRID_W], axis=-1).astype(F32)
    inv_freq = ROPE_THETA ** (-jnp.arange(ROPE_FREQS, dtype=F32) / ROPE_FREQS)
    ang = pos[:, :, None] * inv_freq
    ang = jnp.concatenate([ang, jnp.zeros((TAIL, 2, ROPE_FREQS), F32)], axis=0)
    cos, sin = jnp.cos(ang), jnp.sin(ang)
    cos64 = jnp.concatenate([cos[:, 0], cos[:, 0], cos[:, 1], cos[:, 1]], axis=-1)
    sin64 = jnp.concatenate([-sin[:, 0], sin[:, 0], -sin[:, 1], sin[:, 1]], axis=-1)
    return jnp.tile(cos64, (1, 2)), jnp.tile(sin64, (1, 2))


def _block_ones(width):
    idx = jnp.arange(width) // HEAD_DIM
    return (idx[:, None] == idx[None, :]).astype(BF16)


def _layer_params(l, w_in, shift_mu, decay_w0, decay_up, iclr_a0, iclr_up, gate_up, k_k, k_a, r_k,
                  gn_g, gn_b, q_gain, k_gain, w_out, ln1_g, ln1_b, w_ffn_in, w_ffn_out, ln2_g, ln2_b):
    def lora(up, row0):
        rows = up.shape[-2]
        full = jnp.zeros(up.shape[:-2] + (LORA_WIDTH, RWKV_WIDTH), F32)
        return full.at[..., row0:row0 + rows, :].set(up).astype(BF16)

    row = lambda a: a.reshape(1, -1)
    return dict(
        w_in=w_in[l].astype(BF16),
        mu=shift_mu[l],
        w0=decay_w0[l].reshape(2, 1, RWKV_WIDTH),
        w1=lora(decay_up[l], 0),
        a0=iclr_a0[l].reshape(2, 1, RWKV_WIDTH),
        w2=lora(iclr_up[l], 32),
        w3=lora(gate_up[l], 64),
        k_k=row(k_k[l]), k_a=row(k_a[l]), r_k=row(r_k[l]), gn_g=row(gn_g[l]), gn_b=row(gn_b[l]),
        q_gain=jnp.tile(row(q_gain[l]), (1, 2)), k_gain=jnp.tile(row(k_gain[l]), (1, 2)),
        wo_r=w_out[l, :RWKV_WIDTH].astype(BF16), wo_a=w_out[l, RWKV_WIDTH:].astype(BF16),
        ln1_g=row(ln1_g[l]), ln1_b=row(ln1_b[l]),
        w_gate=w_ffn_in[l, :, :D_FF].astype(BF16), w_up=w_ffn_in[l, :, D_FF:].astype(BF16),
        w_down=w_ffn_out[l].astype(BF16),
        ln2_g=row(ln2_g[l]), ln2_b=row(ln2_b[l]),
    )


def _trunk(x, meta_tokens, ln_in_g, ln_in_b, layers, ones256, ones128):
    B, N, D = x.shape
    Tp = N + TAIL
    cos_t, sin_t = _rope_tables(N)
    h = _embed(x, meta_tokens, ln_in_g, ln_in_b)
    for lp in layers:
        pr, q, kv = _proj(h.reshape(B * Tp, D), lp["w_in"])
        pr = pr.reshape(B, Tp, RWKV_IN_WIDTH)
        o_bwd = _rwkv_dir(pr, lp, ones256, 1)
        y_rwkv = _rwkv_dir(pr, lp, ones256, 0, o_bwd)
        y_attn = _attention(q.reshape(B, Tp, ATTN_WIDTH), kv.reshape(B, Tp, 2 * KV_WIDTH),
                            cos_t, sin_t, lp["q_gain"], lp["k_gain"], ones128)
        h = _post(h.reshape(B * Tp, D), y_rwkv.reshape(B * Tp, RWKV_WIDTH),
                  y_attn.reshape(B * Tp, ATTN_WIDTH), lp).reshape(B, Tp, D)
    return h[:, :N]


def kernel(x_prompt, x_sample, meta_tokens, ln_in_g, ln_in_b, w_in, shift_mu, decay_w0, decay_up,
           iclr_a0, iclr_up, gate_up, k_k, k_a, r_k, gn_g, gn_b, q_gain, k_gain, w_out,
           ln1_g, ln1_b, w_ffn_in, w_ffn_out, ln2_g, ln2_b):
    layers = [
        _layer_params(l, w_in, shift_mu, decay_w0, decay_up, iclr_a0, iclr_up, gate_up, k_k, k_a, r_k,
                      gn_g, gn_b, q_gain, k_gain, w_out, ln1_g, ln1_b, w_ffn_in, w_ffn_out, ln2_g, ln2_b)
        for l in range(w_in.shape[0])
    ]
    ones256 = _block_ones(GROUP)
    ones128 = _block_ones(LANES)
    y_prompt = _trunk(x_prompt, meta_tokens, ln_in_g, ln_in_b, layers, ones256, ones128)
    y_sample = _trunk(x_sample, meta_tokens, ln_in_g, ln_in_b, layers, ones256, ones128)
    return (y_prompt, y_sample)
```

```python
import functools
import math

import jax
import jax.numpy as jnp
from jax import lax
from jax.experimental import pallas as pl
from jax.experimental.pallas import tpu as pltpu

F32 = jnp.float32
BF16 = jnp.bfloat16

D_MODEL = 1024
DEPTH = 4
HEAD_DIM = 64
RWKV_WIDTH = 512
ATTN_WIDTH = 512
KV_WIDTH = 128
LORA_WIDTH = 128
RWKV_IN_WIDTH = 3 * RWKV_WIDTH + LORA_WIDTH
IN_WIDTH = RWKV_IN_WIDTH + ATTN_WIDTH + 2 * KV_WIDTH
D_FF = 2816
FF_CHUNK = 1408
N_META = 16
GRID_W = 64
ROPE_THETA = 10000.0
ROPE_FREQS = 16
DEEPNORM_ALPHA = (2.0 * DEPTH) ** 0.25
LN_EPS = 1e-5
GN_EPS = 64e-5
QK_EPS = 1e-6
DECAY_SCALE = math.exp(-0.5)
LOG2_E = math.log2(math.e)

LANES = 128
MXU_TILE = 256
CHUNK = 64
SUPER = 2 * CHUNK
SEQS_PER_STEP = 2
GROUP = 4 * HEAD_DIM
HEADS_PER_GROUP = GROUP // HEAD_DIM
TAIL = 128
PAD = TAIL - N_META
ROW_TILE = 512
Q_TILE = 256
K_PREP_ROWS = 512
NEG_BIG = -1e30
VMEM_LIMIT = 56 * 1024 * 1024

assert GROUP == MXU_TILE and RWKV_WIDTH == 2 * GROUP and SUPER == LANES


def _dot(a, b):
    return jnp.dot(a, b, preferred_element_type=F32)


def _dot_nt(a, b):
    return lax.dot_general(a, b, (((1,), (1,)), ((), ())), preferred_element_type=F32)


def _split3(x):
    hi = x.astype(BF16)
    r1 = x - hi.astype(F32)
    mid = r1.astype(BF16)
    lo = (r1 - mid.astype(F32)).astype(BF16)
    return hi, mid, lo


def _dot3_right(x, m):
    hi, mid, lo = _split3(x)
    return _dot(hi, m) + _dot(mid, m) + _dot(lo, m)


def _dot2_right(x, m):
    hi = x.astype(BF16)
    lo = (x - hi.astype(F32)).astype(BF16)
    return _dot(hi, m) + _dot(lo, m)


def _dot3_left(m, x):
    hi, mid, lo = _split3(x)
    return _dot(m, hi) + _dot(m, mid) + _dot(m, lo)


def _layer_norm(x, g, b):
    mu = jnp.mean(x, axis=-1, keepdims=True)
    xc = x - mu
    var = jnp.mean(xc * xc, axis=-1, keepdims=True)
    return xc * lax.rsqrt(var + LN_EPS) * g + b


def _sigmoid(x):
    return 0.5 * jnp.tanh(0.5 * x) + 0.5


def _pick_tile(total, pref):
    t = pref
    while total % t:
        t //= 2
    return t


def _embed_kernel(x_ref, meta_ref, g_ref, b_ref, o_ref, *, n_x_tiles):
    i = pl.program_id(1)

    @pl.when(i < n_x_tiles)
    def _():
        o_ref[0] = _layer_norm(x_ref[0], g_ref[...], b_ref[...])

    @pl.when(i == n_x_tiles)
    def _():
        o_ref[0] = jnp.zeros(o_ref.shape[1:], F32)
        o_ref[0, PAD:TAIL, :] = _layer_norm(meta_ref[...], g_ref[...], b_ref[...])


def _embed(x, meta, g, b):
    B, N, D = x.shape
    te = _pick_tile(N, ROW_TILE)
    n_x_tiles = N // te
    assert te >= TAIL
    return pl.pallas_call(
        functools.partial(_embed_kernel, n_x_tiles=n_x_tiles),
        grid=(B, n_x_tiles + 1),
        in_specs=[
            pl.BlockSpec((1, te, D), lambda bi, i: (bi, jnp.minimum(i, n_x_tiles - 1), 0)),
            pl.BlockSpec((N_META, D), lambda bi, i: (0, 0)),
            pl.BlockSpec((1, D), lambda bi, i: (0, 0)),
            pl.BlockSpec((1, D), lambda bi, i: (0, 0)),
        ],
        out_specs=pl.BlockSpec((1, te, D), lambda bi, i: (bi, i, 0)),
        out_shape=jax.ShapeDtypeStruct((B, N + TAIL, D), F32),
        compiler_params=pltpu.CompilerParams(dimension_semantics=("arbitrary", "arbitrary")),
        name="embed",
    )(x, meta, g.reshape(1, D), b.reshape(1, D))


def _proj_kernel(h_ref, w_ref, pr_ref, q_ref, kv_ref, *, rows_per_seq):
    tm = h_ref.shape[0]
    y = _dot(h_ref[...].astype(BF16), w_ref[...])
    row0 = pl.program_id(0) * tm
    pad0 = lax.div(row0, rows_per_seq) * rows_per_seq + (rows_per_seq - TAIL) - row0
    row = lax.broadcasted_iota(jnp.int32, (tm, 1), 0)
    is_pad = jnp.logical_and(row >= pad0, row < pad0 + PAD)
    pr_ref[...] = jnp.where(is_pad, 0.0, y[:, :RWKV_IN_WIDTH])
    q_ref[...] = y[:, RWKV_IN_WIDTH:RWKV_IN_WIDTH + ATTN_WIDTH]
    kv_ref[...] = y[:, RWKV_IN_WIDTH + ATTN_WIDTH:]


def _proj(h2, w_in_b, rows_per_seq):
    M, D = h2.shape
    tm = _pick_tile(M, ROW_TILE)
    assert tm <= rows_per_seq - TAIL
    return pl.pallas_call(
        functools.partial(_proj_kernel, rows_per_seq=rows_per_seq),
        grid=(M // tm,),
        in_specs=[
            pl.BlockSpec((tm, D), lambda i: (i, 0)),
            pl.BlockSpec((D, IN_WIDTH), lambda i: (0, 0)),
        ],
        out_specs=[
            pl.BlockSpec((tm, RWKV_IN_WIDTH), lambda i: (i, 0)),
            pl.BlockSpec((tm, ATTN_WIDTH), lambda i: (i, 0)),
            pl.BlockSpec((tm, 2 * KV_WIDTH), lambda i: (i, 0)),
        ],
        out_shape=[
            jax.ShapeDtypeStruct((M, RWKV_IN_WIDTH), F32),
            jax.ShapeDtypeStruct((M, ATTN_WIDTH), F32),
            jax.ShapeDtypeStruct((M, 2 * KV_WIDTH), F32),
        ],
        compiler_params=pltpu.CompilerParams(
            dimension_semantics=("arbitrary",), vmem_limit_bytes=VMEM_LIMIT),
        name="proj",
    )(h2, w_in_b)


def _rwkv_kernel(*refs, direction, n_super):
    z = direction
    if z == 0:
        (pr_ref, prev_ref, next_ref, ob_ref, mu_ref, w0_ref, w1_ref, a0_ref, w2_ref, a0o_ref, w2o_ref,
         w3_ref, kk_ref, ka_ref, rk_ref, gg_ref, gb_ref, ones_ref, out_ref, hs_ref, xs_ref) = refs
    else:
        (pr_ref, prev_ref, next_ref, mu_ref, w0_ref, w1_ref, a0_ref, w2_ref,
         kk_ref, ka_ref, ones_ref, out_ref, hs_ref, xs_ref) = refs
    L, R, NB = CHUNK, SUPER, SEQS_PER_STEP
    RR = NB * R
    n_ch = RR // L
    sign = 1 - 2 * z
    c = pl.program_id(1)
    cs = c if z == 0 else n_super - 1 - c
    sj = lax.rem(cs + n_super - 1, n_super)

    @pl.when(c == 0)
    def _():
        hs_ref[...] = jnp.zeros(hs_ref.shape, F32)

    for sq in range(NB):
        xs_ref[sq, 0:8, :] = prev_ref[sq]
        xs_ref[sq, 8:8 + R, :] = pr_ref[sq]
        xs_ref[sq, 8 + R:16 + R, :] = next_ref[sq]
    p = jnp.concatenate([pr_ref[sq] for sq in range(NB)], axis=0)
    prev = jnp.concatenate([xs_ref[sq, 7:7 + R, :] for sq in range(NB)], axis=0)
    nxt = jnp.concatenate([xs_ref[sq, 9:9 + R, :] for sq in range(NB)], axis=0)
    mu = mu_ref[...]
    ps = (1.0 - mu[0:1] - mu[1:2]) * p + mu[0:1] * prev + mu[1:2] * nxt

    row = lax.broadcasted_iota(jnp.int32, (RR, 1), 0) % R
    valid = jnp.logical_or(sj != n_super - 1, row >= PAD)

    r = ps[:, 0:RWKV_WIDTH]
    k = jnp.where(valid, ps[:, RWKV_WIDTH:2 * RWKV_WIDTH], 0.0)
    v = ps[:, 2 * RWKV_WIDTH:3 * RWKV_WIDTH]
    lo = ps[:, 3 * RWKV_WIDTH:]
    lo_b = lo.astype(BF16)
    ones_b = ones_ref[...]

    def segsum(x):
        return jnp.concatenate([_dot2_right(x[:, :GROUP], ones_b), _dot2_right(x[:, GROUP:], ones_b)], axis=1)

    w = w0_ref[...] + _dot(jnp.tanh(lo).astype(BF16), w1_ref[...])
    lw = jnp.where(valid, -DECAY_SCALE * _sigmoid(w), 0.0)
    a_z = _sigmoid(a0_ref[...] + _dot(lo_b, w2_ref[...]))
    kkv = k * kk_ref[...]
    kkn = kkv * jnp.minimum(lax.rsqrt(segsum(kkv * kkv)), 1e12)
    kdir = k * (1.0 + (a_z - 1.0) * ka_ref[...])

    ti = lax.broadcasted_iota(jnp.int32, (RR, RR), 0)
    si = lax.broadcasted_iota(jnp.int32, (RR, RR), 1)
    tri = jnp.where(jnp.logical_and(ti // L == si // L, (ti - si) * sign >= 0), 1.0, 0.0).astype(BF16)
    c_inc = _dot3_left(tri, lw)
    last = L - 1 if z == 0 else 0
    ctot = [c_inc[ch * L + last:ch * L + last + 1, :] for ch in range(n_ch)]
    ctot_rows = jnp.concatenate([jnp.broadcast_to(ct, (L, RWKV_WIDTH)) for ct in ctot], axis=0)
    e_neg = jnp.exp(-c_inc)
    g_rel = jnp.exp(ctot_rows - c_inc)
    rh = r * jnp.exp(c_inc)
    ah = -kkn * jnp.exp(c_inc - lw)
    kka = kkn * a_z
    bh = kka * e_neg
    kh = kdir * e_neg
    bg = kka * g_rel
    kg = kdir * g_rel

    bkt = [jnp.concatenate([bg[ch * L:(ch + 1) * L], kg[ch * L:(ch + 1) * L]], axis=0).T for ch in range(n_ch)]
    gam = [jnp.broadcast_to(jnp.exp(ct), (LANES, RWKV_WIDTH)).T for ct in ctot]

    lane_g = lax.broadcasted_iota(jnp.int32, (1, GROUP), 1)
    hm = [lane_g // HEAD_DIM == i for i in range(HEADS_PER_GROUP)]
    d4 = (lax.broadcasted_iota(jnp.int32, (L, GROUP), 0)
          - lax.broadcasted_iota(jnp.int32, (L, GROUP), 1) % L) * sign
    strict4 = d4 > 0
    incl4 = d4 >= 0
    eye4 = jnp.where(d4 == 0, 1.0, 0.0)
    bdmask = (lax.broadcasted_iota(jnp.int32, (GROUP, GROUP), 0) // HEAD_DIM
              == lax.broadcasted_iota(jnp.int32, (GROUP, GROUP), 1) // HEAD_DIM)
    zb = jnp.zeros((), BF16)

    def blocks(x4b):
        return [jnp.where(hm[i], x4b, zb) for i in range(HEADS_PER_GROUP)]

    def bd(x4b):
        return jnp.concatenate(blocks(x4b), axis=0)

    probs = [(ch, gq) for ch in range(n_ch) for gq in range(2)]

    def sl(pb):
        ch, gq = pb
        return slice(ch * L, (ch + 1) * L), slice(gq * GROUP, (gq + 1) * GROUP)

    aab, aak, arb, ark = {}, {}, {}, {}
    for pb in probs:
        rs, ls = sl(pb)
        lhs = jnp.concatenate([ah[rs, ls], rh[rs, ls]], axis=0).astype(BF16)
        rhs = jnp.concatenate(blocks(bh[rs, ls].astype(BF16)) + blocks(kh[rs, ls].astype(BF16)), axis=0)
        g = _dot_nt(lhs, rhs)
        aab[pb] = jnp.where(strict4, g[:L, :GROUP], 0.0)
        aak[pb] = jnp.where(strict4, g[:L, GROUP:], 0.0)
        arb[pb] = jnp.where(incl4, g[L:, :GROUP], 0.0)
        ark[pb] = jnp.where(incl4, g[L:, GROUP:], 0.0)

    vblk, wv, pw, tinv = {}, {}, {}, {}
    for pb in probs:
        rs, ls = sl(pb)
        vblk[pb] = blocks(v[rs, ls].astype(BF16))
        aab_b = aab[pb].astype(BF16)
        wv[pb] = _dot(aak[pb].astype(BF16), jnp.concatenate(vblk[pb], axis=0))
        pw[pb] = _dot(aab_b, bd(aab_b))
        tinv[pb] = eye4 + aab[pb]

    for _ in range(4):
        for pb in probs:
            p_b = pw[pb].astype(BF16)
            res = _dot(jnp.concatenate([tinv[pb].astype(BF16), p_b], axis=0), bd(p_b))
            tinv[pb] = tinv[pb] + res[:L]
            pw[pb] = res[L:]
    for pb in probs:
        tinv[pb] = tinv[pb] + _dot(tinv[pb].astype(BF16), bd(pw[pb].astype(BF16)))

    t_b, u_b = {}, {}
    for pb in probs:
        rs, ls = sl(pb)
        rhs = jnp.concatenate([bd(ah[rs, ls].astype(BF16)), bd(wv[pb].astype(BF16))], axis=1)
        tu = _dot(tinv[pb].astype(BF16), rhs)
        t_b[pb] = tu[:, :GROUP].astype(BF16)
        u_b[pb] = tu[:, GROUP:].astype(BF16)

    omega, oloc, pc, qq = {}, {}, {}, {}
    zero_blk = jnp.zeros((L, GROUP), BF16)
    for pb in probs:
        ch, gq = pb
        rs, ls = sl(pb)
        tblk, ublk = blocks(t_b[pb]), blocks(u_b[pb])
        omega[pb] = rh[rs, ls] + _dot(arb[pb].astype(BF16), jnp.concatenate(tblk, axis=0))
        oloc[pb] = _dot(jnp.concatenate([arb[pb], ark[pb]], axis=1).astype(BF16),
                        jnp.concatenate(ublk + vblk[pb], axis=0))
        bkt4 = jnp.concatenate(
            [bkt[ch][gq * GROUP + i * HEAD_DIM:gq * GROUP + (i + 1) * HEAD_DIM, :]
             for i in range(HEADS_PER_GROUP)], axis=1).astype(BF16)
        rhs_p, rhs_q = [], []
        for i in range(HEADS_PER_GROUP):
            rhs_p += [tblk[i], zero_blk]
            rhs_q += [ublk[i], vblk[pb][i]]
        pq = _dot(bkt4, jnp.concatenate([jnp.concatenate(rhs_p, axis=0), jnp.concatenate(rhs_q, axis=0)], axis=1))
        pc[pb] = pq[:, :GROUP]
        qq[pb] = pq[:, GROUP:]

    o4 = {}
    hbd = {(sq, gq): hs_ref[sq, gq] for sq in range(NB) for gq in range(2)}
    for local in ((0, 1) if z == 0 else (1, 0)):
        for sq in range(NB):
            for gq in range(2):
                ch = sq * (R // L) + local
                pb = (ch, gq)
                hb = hbd[(sq, gq)].astype(BF16)
                o4[pb] = _dot(omega[pb].astype(BF16), hb) + oloc[pb]
                upd = _dot(pc[pb].astype(BF16), hb) + qq[pb]
                g_rows = gam[ch][gq * GROUP:(gq + 1) * GROUP, :]
                hbd[(sq, gq)] = (jnp.concatenate([g_rows, g_rows], axis=1) * hbd[(sq, gq)]
                                 + jnp.where(bdmask, jnp.concatenate([upd] * HEADS_PER_GROUP, axis=0), 0.0))
    for (sq, gq), hval in hbd.items():
        hs_ref[sq, gq] = hval
    o = jnp.concatenate([jnp.concatenate([o4[(ch, 0)], o4[(ch, 1)]], axis=1) for ch in range(n_ch)], axis=0)

    if z == 1:
        for sq in range(NB):
            out_ref[sq] = o[sq * R:(sq + 1) * R]
    else:
        osum = o + jnp.concatenate([ob_ref[sq] for sq in range(NB)], axis=0)
        inv_hd = 1.0 / HEAD_DIM
        mo = segsum(osum) * inv_hd
        dlt = osum - mo
        vo = segsum(dlt * dlt) * inv_hd
        on = dlt * lax.rsqrt(vo + GN_EPS) * gg_ref[...] + gb_ref[...]
        a_bwd = _sigmoid(a0o_ref[...] + _dot(lo_b, w2o_ref[...]))
        kdir_bwd = k * (1.0 + (a_bwd - 1.0) * ka_ref[...])
        bonus = segsum(r * (kdir + kdir_bwd) * rk_ref[...]) * v
        gate = _dot(_sigmoid(lo).astype(BF16), w3_ref[...])
        y = (on + bonus) * gate
        for sq in range(NB):
            out_ref[sq] = y[sq * R:(sq + 1) * R]


def _rwkv_dir(pr, lp, ones256, direction, o_bwd=None):
    B, Tp, _ = pr.shape
    n_super = Tp // SUPER
    n8 = Tp // 8
    z = direction
    nb = SEQS_PER_STEP
    assert B % nb == 0

    def sblk(c):
        cs = c if z == 0 else n_super - 1 - c
        return lax.rem(cs + n_super - 1, n_super)

    def const(shape):
        return pl.BlockSpec(shape, lambda bi, c: (0,) * len(shape))

    row_spec = lambda width: pl.BlockSpec((nb, SUPER, width), lambda bi, c: (bi, sblk(c), 0))
    vec = const((1, RWKV_WIDTH))
    lora = const((LORA_WIDTH, RWKV_WIDTH))
    in_specs = [
        row_spec(RWKV_IN_WIDTH),
        pl.BlockSpec((nb, 8, RWKV_IN_WIDTH),
                     lambda bi, c: (bi, lax.rem(sblk(c) * (SUPER // 8) + n8 - 1, n8), 0)),
        pl.BlockSpec((nb, 8, RWKV_IN_WIDTH),
                     lambda bi, c: (bi, lax.rem((sblk(c) + 1) * (SUPER // 8), n8), 0)),
    ]
    args = [pr, pr, pr]
    if z == 0:
        in_specs.append(row_spec(RWKV_WIDTH))
        args.append(o_bwd)
    in_specs += [const((2, RWKV_IN_WIDTH)), vec, lora, vec, lora]
    args += [lp["mu"], lp["w0"][z], lp["w1"][z], lp["a0"][z], lp["w2"][z]]
    if z == 0:
        in_specs += [vec, lora, lora]
        args += [lp["a0"][1], lp["w2"][1], lp["w3"]]
    in_specs += [vec, vec]
    args += [lp["k_k"], lp["k_a"]]
    if z == 0:
        in_specs += [vec, vec, vec]
        args += [lp["r_k"], lp["gn_g"], lp["gn_b"]]
    in_specs.append(const((GROUP, GROUP)))
    args.append(ones256)
    return pl.pallas_call(
        functools.partial(_rwkv_kernel, direction=z, n_super=n_super),
        grid=(B // nb, n_super),
        in_specs=in_specs,
        out_specs=row_spec(RWKV_WIDTH),
        out_shape=jax.ShapeDtypeStruct((B, Tp, RWKV_WIDTH), F32),
        scratch_shapes=[
            pltpu.VMEM((nb, 2, GROUP, GROUP), F32),
            pltpu.VMEM((nb, SUPER + 16, RWKV_IN_WIDTH), F32),
        ],
        compiler_params=pltpu.CompilerParams(
            dimension_semantics=("arbitrary", "arbitrary"), vmem_limit_bytes=VMEM_LIMIT),
        name="rwkv_fwd" if z == 0 else "rwkv_bwd",
    )(*args)


def _rms_rope(x, gain, cos, sin, ones_b):
    ms = _dot3_right(x * x, ones_b) * (1.0 / HEAD_DIM)
    xn = x * lax.rsqrt(ms + QK_EPS) * gain
    lane = lax.broadcasted_iota(jnp.int32, (1, LANES), 1)
    first = (lane % (2 * ROPE_FREQS)) < ROPE_FREQS
    partner = jnp.where(first, pltpu.roll(xn, LANES - ROPE_FREQS, axis=1),
                        pltpu.roll(xn, ROPE_FREQS, axis=1))
    return xn * cos + partner * sin


def _attn_kernel(q_ref, kv_ref, cq_ref, sq_ref, ck_ref, sk_ref, qg_ref, kg_ref, ones_ref,
                 o_ref, kt_ref, vv_ref, *, n_real):
    N = n_real
    i = pl.program_id(1)
    ones_b = ones_ref[...]
    lane = lax.broadcasted_iota(jnp.int32, (1, LANES), 1)
    low = lane < HEAD_DIM

    @pl.when(i == 0)
    def _():
        def put(kn, vv, col0, width):
            kt = kn.T
            zero = jnp.zeros((HEAD_DIM, width), F32)
            cols = pl.ds(col0, width)
            kt_ref[0, :, cols] = jnp.concatenate([kt[:HEAD_DIM], zero], axis=0).astype(BF16)
            kt_ref[1, :, cols] = jnp.concatenate([zero, kt[:HEAD_DIM]], axis=0).astype(BF16)
            kt_ref[2, :, cols] = jnp.concatenate([kt[HEAD_DIM:], zero], axis=0).astype(BF16)
            kt_ref[3, :, cols] = jnp.concatenate([zero, kt[HEAD_DIM:]], axis=0).astype(BF16)
            one_hi = jnp.where(lane == HEAD_DIM, 1.0, 0.0)
            one_lo = jnp.where(lane == 0, 1.0, 0.0)
            vsw = pltpu.roll(vv, HEAD_DIM, axis=1)
            vv_ref[0, cols, :] = jnp.where(low, vv, one_hi).astype(BF16)
            vv_ref[1, cols, :] = jnp.where(low, one_lo, vsw).astype(BF16)
            vv_ref[2, cols, :] = jnp.where(low, vsw, one_hi).astype(BF16)
            vv_ref[3, cols, :] = jnp.where(low, one_lo, vv).astype(BF16)

        ck = min(K_PREP_ROWS, N)

        def body(rb, carry):
            r0 = pl.multiple_of(rb * ck, ck)
            rows = pl.ds(r0, ck)
            kvc = kv_ref[0, rows, :]
            kn = _rms_rope(kvc[:, :KV_WIDTH], kg_ref[...], ck_ref[rows, :], sk_ref[rows, :], ones_b)
            put(kn, kvc[:, KV_WIDTH:], r0, ck)
            return carry

        lax.fori_loop(0, N // ck, body, 0)
        kvt = kv_ref[0, N:N + TAIL, :]
        knt = _rms_rope(kvt[:, :KV_WIDTH], kg_ref[...], ck_ref[N:N + TAIL, :], sk_ref[N:N + TAIL, :], ones_b)
        put(knt, kvt[:, KV_WIDTH:], N, TAIL)

    tail_col = lax.broadcasted_iota(jnp.int32, (1, TAIL), 1)
    tail_bias = jnp.where(tail_col >= PAD, 0.0, NEG_BIG)

    q = q_ref[0]
    cq = cq_ref[...]
    sq = sq_ref[...]
    heads = [(jp, e) for jp in range(ATTN_WIDTH // LANES) for e in range(2)]
    qp = [(_rms_rope(q[:, jp * LANES:(jp + 1) * LANES], qg_ref[...], cq, sq, ones_b)
           * (HEAD_DIM ** -0.5 * LOG2_E)).astype(BF16) for jp in range(ATTN_WIDTH // LANES)]

    def scores(jp, e):
        kidx = 2 * ((2 * jp + e) // 4) + e
        return _dot(qp[jp], kt_ref[kidx, :, :N]), _dot(qp[jp], kt_ref[kidx, :, N:]) + tail_bias

    halves = {}
    nxt = scores(*heads[0])
    for n, (jp, e) in enumerate(heads):
        s_main, s_tail = nxt
        if n + 1 < len(heads):
            nxt = scores(*heads[n + 1])
        vidx = 2 * ((2 * jp + e) // 4) + e
        m = jnp.maximum(jnp.max(s_main, axis=-1, keepdims=True), jnp.max(s_tail, axis=-1, keepdims=True))
        acc = (_dot(jnp.exp2(s_main - m).astype(BF16), vv_ref[vidx, :N, :])
               + _dot(jnp.exp2(s_tail - m).astype(BF16), vv_ref[vidx, N:, :]))
        ones_lane = (1 - e) * HEAD_DIM
        halves[(jp, e)] = acc / acc[:, ones_lane:ones_lane + 1]
        if e == 1:
            o_ref[0, :, jp * LANES:(jp + 1) * LANES] = jnp.where(low, halves[(jp, 0)], halves[(jp, 1)])


def _attention(q, kv, cos_t, sin_t, q_gain, k_gain, ones128):
    B, Tp, _ = q.shape
    N = Tp - TAIL
    tq = min(Q_TILE, N)
    nq = N // tq
    nk = N + TAIL
    return pl.pallas_call(
        functools.partial(_attn_kernel, n_real=N),
        grid=(B, nq + 1),
        in_specs=[
            pl.BlockSpec((1, tq, ATTN_WIDTH), lambda bi, i: (bi, i, 0)),
            pl.BlockSpec((1, Tp, 2 * KV_WIDTH), lambda bi, i: (bi, 0, 0)),
            pl.BlockSpec((tq, LANES), lambda bi, i: (i, 0)),
            pl.BlockSpec((tq, LANES), lambda bi, i: (i, 0)),
            pl.BlockSpec((Tp, LANES), lambda bi, i: (0, 0)),
            pl.BlockSpec((Tp, LANES), lambda bi, i: (0, 0)),
            pl.BlockSpec((1, LANES), lambda bi, i: (0, 0)),
            pl.BlockSpec((1, LANES), lambda bi, i: (0, 0)),
            pl.BlockSpec((LANES, LANES), lambda bi, i: (0, 0)),
        ],
        out_specs=pl.BlockSpec((1, tq, ATTN_WIDTH), lambda bi, i: (bi, i, 0)),
        out_shape=jax.ShapeDtypeStruct((B, Tp, ATTN_WIDTH), F32),
        scratch_shapes=[
            pltpu.VMEM((4, LANES, nk), BF16),
            pltpu.VMEM((4, nk, LANES), BF16),
        ],
        compiler_params=pltpu.CompilerParams(
            dimension_semantics=("arbitrary", "arbitrary"), vmem_limit_bytes=VMEM_LIMIT),
        name="attention",
    )(q, kv, cos_t, sin_t, cos_t, sin_t, q_gain, k_gain, ones128)


def _post_kernel(h_ref, yr_ref, ya_ref, wor_ref, woa_ref, g1_ref, b1_ref,
                 wg_ref, wu_ref, wd_ref, g2_ref, b2_ref, o_ref):
    mix = _dot(yr_ref[...].astype(BF16), wor_ref[...]) + _dot(ya_ref[...].astype(BF16), woa_ref[...])
    x1 = _layer_norm(DEEPNORM_ALPHA * h_ref[...] + mix, g1_ref[...], b1_ref[...])
    x1b = x1.astype(BF16)
    ffn = None
    for jc in range(D_FF // FF_CHUNK):
        cols = slice(jc * FF_CHUNK, (jc + 1) * FF_CHUNK)
        gate = _dot(x1b, wg_ref[:, cols])
        up = _dot(x1b, wu_ref[:, cols])
        half = 0.5 * gate
        act = ((half + half * jnp.tanh(half)) * up).astype(BF16)
        part = _dot(act, wd_ref[cols, :])
        ffn = part if ffn is None else ffn + part
    o_ref[...] = _layer_norm(DEEPNORM_ALPHA * x1 + ffn, g2_ref[...], b2_ref[...])


def _post(h2, yr2, ya2, lp):
    M, D = h2.shape
    tm = _pick_tile(M, ROW_TILE)

    def const(shape):
        return pl.BlockSpec(shape, lambda i: (0, 0), pipeline_mode=pl.Buffered(1))

    return pl.pallas_call(
        _post_kernel,
        grid=(M // tm,),
        in_specs=[
            pl.BlockSpec((tm, D), lambda i: (i, 0)),
            pl.BlockSpec((tm, RWKV_WIDTH), lambda i: (i, 0)),
            pl.BlockSpec((tm, ATTN_WIDTH), lambda i: (i, 0)),
            const((RWKV_WIDTH, D)),
            const((ATTN_WIDTH, D)),
            const((1, D)),
            const((1, D)),
            const((D, D_FF)),
            const((D, D_FF)),
            const((D_FF, D)),
            const((1, D)),
            const((1, D)),
        ],
        out_specs=pl.BlockSpec((tm, D), lambda i: (i, 0)),
        out_shape=jax.ShapeDtypeStruct((M, D), F32),
        compiler_params=pltpu.CompilerParams(
            dimension_semantics=("arbitrary",), vmem_limit_bytes=VMEM_LIMIT),
        name="post",
    )(h2, yr2, ya2, lp["wo_r"], lp["wo_a"], lp["ln1_g"], lp["ln1_b"],
      lp["w_gate"], lp["w_up"], lp["w_down"], lp["ln2_g"], lp["ln2_b"])


def _rope_tables(n_real):
    tok = jnp.arange(n_real)
    pos = jnp.stack([tok // GRID_W, tok % GRID_W], axis=-1).astype(F32)
    inv_freq = ROPE_THETA ** (-jnp.arange(ROPE_FREQS, dtype=F32) / ROPE_FREQS)
    ang = pos[:, :, None] * inv_freq
    ang = jnp.concatenate([ang, jnp.zeros((TAIL, 2, ROPE_FREQS), F32)], axis=0)
    cos, sin = jnp.cos(ang), jnp.sin(ang)
    cos64 = jnp.concatenate([cos[:, 0], cos[:, 0], cos[:, 1], cos[:, 1]], axis=-1)
    sin64 = jnp.concatenate([-sin[:, 0], sin[:, 0], -sin[:, 1], sin[:, 1]], axis=-1)
    return jnp.tile(cos64, (1, 2)), jnp.tile(sin64, (1, 2))


def _block_ones(width):
    idx = jnp.arange(width) // HEAD_DIM
    return (idx[:, None] == idx[None, :]).astype(BF16)


def _layer_params(l, w_in, shift_mu, decay_w0, decay_up, iclr_a0, iclr_up, gate_up, k_k, k_a, r_k,
                  gn_g, gn_b, q_gain, k_gain, w_out, ln1_g, ln1_b, w_ffn_in, w_ffn_out, ln2_g, ln2_b):
    def lora(up, row0):
        rows = up.shape[-2]
        full = jnp.zeros(up.shape[:-2] + (LORA_WIDTH, RWKV_WIDTH), F32)
        return full.at[..., row0:row0 + rows, :].set(up).astype(BF16)

    row = lambda a: a.reshape(1, -1)
    return dict(
        w_in=w_in[l].astype(BF16),
        mu=shift_mu[l],
        w0=decay_w0[l].reshape(2, 1, RWKV_WIDTH),
        w1=lora(decay_up[l], 0),
        a0=iclr_a0[l].reshape(2, 1, RWKV_WIDTH),
        w2=lora(iclr_up[l], 32),
        w3=lora(gate_up[l], 64),
        k_k=row(k_k[l]), k_a=row(k_a[l]), r_k=row(r_k[l]), gn_g=row(gn_g[l]), gn_b=row(gn_b[l]),
        q_gain=jnp.tile(row(q_gain[l]), (1, 2)), k_gain=jnp.tile(row(k_gain[l]), (1, 2)),
        wo_r=w_out[l, :RWKV_WIDTH].astype(BF16), wo_a=w_out[l, RWKV_WIDTH:].astype(BF16),
        ln1_g=row(ln1_g[l]), ln1_b=row(ln1_b[l]),
        w_gate=w_ffn_in[l, :, :D_FF].astype(BF16), w_up=w_ffn_in[l, :, D_FF:].astype(BF16),
        w_down=w_ffn_out[l].astype(BF16),
        ln2_g=row(ln2_g[l]), ln2_b=row(ln2_b[l]),
    )


def _trunk(x, meta_tokens, ln_in_g, ln_in_b, layers, ones256, ones128):
    B, N, D = x.shape
    Tp = N + TAIL
    cos_t, sin_t = _rope_tables(N)
    h = _embed(x, meta_tokens, ln_in_g, ln_in_b)
    for lp in layers:
        pr, q, kv = _proj(h.reshape(B * Tp, D), lp["w_in"], Tp)
        pr = pr.reshape(B, Tp, RWKV_IN_WIDTH)
        o_bwd = _rwkv_dir(pr, lp, ones256, 1)
        y_rwkv = _rwkv_dir(pr, lp, ones256, 0, o_bwd)
        y_attn = _attention(q.reshape(B, Tp, ATTN_WIDTH), kv.reshape(B, Tp, 2 * KV_WIDTH),
                            cos_t, sin_t, lp["q_gain"], lp["k_gain"], ones128)
        h = _post(h.reshape(B * Tp, D), y_rwkv.reshape(B * Tp, RWKV_WIDTH),
                  y_attn.reshape(B * Tp, ATTN_WIDTH), lp).reshape(B, Tp, D)
    return h[:, :N]


def kernel(x_prompt, x_sample, meta_tokens, ln_in_g, ln_in_b, w_in, shift_mu, decay_w0, decay_up,
           iclr_a0, iclr_up, gate_up, k_k, k_a, r_k, gn_g, gn_b, q_gain, k_gain, w_out,
           ln1_g, ln1_b, w_ffn_in, w_ffn_out, ln2_g, ln2_b):
    layers = [
        _layer_params(l, w_in, shift_mu, decay_w0, decay_up, iclr_a0, iclr_up, gate_up, k_k, k_a, r_k,
                      gn_g, gn_b, q_gain, k_gain, w_out, ln1_g, ln1_b, w_ffn_in, w_ffn_out, ln2_g, ln2_b)
        for l in range(w_in.shape[0])
    ]
    ones256 = _block_ones(GROUP)
    ones128 = _block_ones(LANES)
    y_prompt = _trunk(x_prompt, meta_tokens, ln_in_g, ln_in_b, layers, ones256, ones128)
    y_sample = _trunk(x_sample, meta_tokens, ln_in_g, ln_in_b, layers, ones256, ones128)
    return (y_prompt, y_sample)
```

```python
import functools
import math

import jax
import jax.numpy as jnp
from jax import lax
from jax.experimental import pallas as pl
from jax.experimental.pallas import tpu as pltpu

F32 = jnp.float32
BF16 = jnp.bfloat16

D_MODEL = 1024
DEPTH = 4
HEAD_DIM = 64
RWKV_WIDTH = 512
ATTN_WIDTH = 512
KV_WIDTH = 128
LORA_WIDTH = 128
RWKV_IN_WIDTH = 3 * RWKV_WIDTH + LORA_WIDTH
IN_WIDTH = RWKV_IN_WIDTH + ATTN_WIDTH + 2 * KV_WIDTH
D_FF = 2816
FF_CHUNK = 1408
N_META = 16
GRID_W = 64
ROPE_THETA = 10000.0
ROPE_FREQS = 16
DEEPNORM_ALPHA = (2.0 * DEPTH) ** 0.25
LN_EPS = 1e-5
GN_EPS = 64e-5
QK_EPS = 1e-6
DECAY_SCALE = math.exp(-0.5)
LOG2_E = math.log2(math.e)

LANES = 128
MXU_TILE = 256
CHUNK = 64
SUPER = 2 * CHUNK
SEQS_PER_STEP = 2
GROUP = 4 * HEAD_DIM
HEADS_PER_GROUP = GROUP // HEAD_DIM
TAIL = 128
PAD = TAIL - N_META
ROW_TILE = 512
Q_TILE = 256
K_PREP_ROWS = 512
NEG_BIG = -1e30
VMEM_LIMIT = 56 * 1024 * 1024

assert GROUP == MXU_TILE and RWKV_WIDTH == 2 * GROUP and SUPER == LANES


def _dot(a, b):
    return jnp.dot(a, b, preferred_element_type=F32)


def _dot_nt(a, b):
    return lax.dot_general(a, b, (((1,), (1,)), ((), ())), preferred_element_type=F32)


def _split3(x):
    hi = x.astype(BF16)
    r1 = x - hi.astype(F32)
    mid = r1.astype(BF16)
    lo = (r1 - mid.astype(F32)).astype(BF16)
    return hi, mid, lo


def _dot3_right(x, m):
    hi, mid, lo = _split3(x)
    return _dot(hi, m) + _dot(mid, m) + _dot(lo, m)


def _dot2_right(x, m):
    hi = x.astype(BF16)
    lo = (x - hi.astype(F32)).astype(BF16)
    return _dot(hi, m) + _dot(lo, m)


def _dot3_left(m, x):
    hi, mid, lo = _split3(x)
    return _dot(m, hi) + _dot(m, mid) + _dot(m, lo)


def _layer_norm(x, g, b):
    mu = jnp.mean(x, axis=-1, keepdims=True)
    xc = x - mu
    var = jnp.mean(xc * xc, axis=-1, keepdims=True)
    return xc * lax.rsqrt(var + LN_EPS) * g + b


def _sigmoid(x):
    return 0.5 * jnp.tanh(0.5 * x) + 0.5


def _pick_tile(total, pref):
    t = pref
    while total % t:
        t //= 2
    return t


def _embed_kernel(x_ref, meta_ref, g_ref, b_ref, o_ref, *, n_x_tiles):
    i = pl.program_id(1)

    @pl.when(i < n_x_tiles)
    def _():
        o_ref[0] = _layer_norm(x_ref[0], g_ref[...], b_ref[...])

    @pl.when(i == n_x_tiles)
    def _():
        o_ref[0] = jnp.zeros(o_ref.shape[1:], F32)
        o_ref[0, PAD:TAIL, :] = _layer_norm(meta_ref[...], g_ref[...], b_ref[...])


def _embed(x, meta, g, b):
    B, N, D = x.shape
    te = _pick_tile(N, ROW_TILE)
    n_x_tiles = N // te
    assert te >= TAIL
    return pl.pallas_call(
        functools.partial(_embed_kernel, n_x_tiles=n_x_tiles),
        grid=(B, n_x_tiles + 1),
        in_specs=[
            pl.BlockSpec((1, te, D), lambda bi, i: (bi, jnp.minimum(i, n_x_tiles - 1), 0)),
            pl.BlockSpec((N_META, D), lambda bi, i: (0, 0)),
            pl.BlockSpec((1, D), lambda bi, i: (0, 0)),
            pl.BlockSpec((1, D), lambda bi, i: (0, 0)),
        ],
        out_specs=pl.BlockSpec((1, te, D), lambda bi, i: (bi, i, 0)),
        out_shape=jax.ShapeDtypeStruct((B, N + TAIL, D), F32),
        compiler_params=pltpu.CompilerParams(dimension_semantics=("arbitrary", "arbitrary")),
        name="embed",
    )(x, meta, g.reshape(1, D), b.reshape(1, D))


def _proj_kernel(h_ref, w_ref, pr_ref, q_ref, kv_ref, *, rows_per_seq):
    tm = h_ref.shape[0]
    y = _dot(h_ref[...].astype(BF16), w_ref[...])
    row0 = pl.program_id(0) * tm
    pad0 = lax.div(row0, rows_per_seq) * rows_per_seq + (rows_per_seq - TAIL) - row0
    row = lax.broadcasted_iota(jnp.int32, (tm, 1), 0)
    is_pad = jnp.logical_and(row >= pad0, row < pad0 + PAD)
    pr_ref[...] = jnp.where(is_pad, 0.0, y[:, :RWKV_IN_WIDTH])
    q_ref[...] = y[:, RWKV_IN_WIDTH:RWKV_IN_WIDTH + ATTN_WIDTH]
    kv_ref[...] = y[:, RWKV_IN_WIDTH + ATTN_WIDTH:]


def _proj(h2, w_in_b, rows_per_seq):
    M, D = h2.shape
    tm = _pick_tile(M, ROW_TILE)
    assert tm <= rows_per_seq - TAIL
    return pl.pallas_call(
        functools.partial(_proj_kernel, rows_per_seq=rows_per_seq),
        grid=(M // tm,),
        in_specs=[
            pl.BlockSpec((tm, D), lambda i: (i, 0)),
            pl.BlockSpec((D, IN_WIDTH), lambda i: (0, 0)),
        ],
        out_specs=[
            pl.BlockSpec((tm, RWKV_IN_WIDTH), lambda i: (i, 0)),
            pl.BlockSpec((tm, ATTN_WIDTH), lambda i: (i, 0)),
            pl.BlockSpec((tm, 2 * KV_WIDTH), lambda i: (i, 0)),
        ],
        out_shape=[
            jax.ShapeDtypeStruct((M, RWKV_IN_WIDTH), F32),
            jax.ShapeDtypeStruct((M, ATTN_WIDTH), F32),
            jax.ShapeDtypeStruct((M, 2 * KV_WIDTH), F32),
        ],
        compiler_params=pltpu.CompilerParams(
            dimension_semantics=("arbitrary",), vmem_limit_bytes=VMEM_LIMIT),
        name="proj",
    )(h2, w_in_b)


def _rwkv_kernel(*refs, direction, n_super):
    z = direction
    if z == 0:
        (pr_ref, prev_ref, next_ref, ob_ref, mu_ref, w0_ref, w1_ref, a0_ref, w2_ref, a0o_ref, w2o_ref,
         w3_ref, kk_ref, ka_ref, rk_ref, gg_ref, gb_ref, ones_ref, out_ref, hs_ref, xs_ref) = refs
    else:
        (pr_ref, prev_ref, next_ref, mu_ref, w0_ref, w1_ref, a0_ref, w2_ref,
         kk_ref, ka_ref, ones_ref, out_ref, hs_ref, xs_ref) = refs
    L, R, NB = CHUNK, SUPER, SEQS_PER_STEP
    RR = NB * R
    n_ch = RR // L
    sign = 1 - 2 * z
    c = pl.program_id(1)
    cs = c if z == 0 else n_super - 1 - c
    sj = lax.rem(cs + n_super - 1, n_super)

    @pl.when(c == 0)
    def _():
        hs_ref[...] = jnp.zeros(hs_ref.shape, F32)

    for sq in range(NB):
        xs_ref[sq, 0:8, :] = prev_ref[sq]
        xs_ref[sq, 8:8 + R, :] = pr_ref[sq]
        xs_ref[sq, 8 + R:16 + R, :] = next_ref[sq]
    p = jnp.concatenate([pr_ref[sq] for sq in range(NB)], axis=0)
    prev = jnp.concatenate([xs_ref[sq, 7:7 + R, :] for sq in range(NB)], axis=0)
    nxt = jnp.concatenate([xs_ref[sq, 9:9 + R, :] for sq in range(NB)], axis=0)
    mu = mu_ref[...]
    ps = (1.0 - mu[0:1] - mu[1:2]) * p + mu[0:1] * prev + mu[1:2] * nxt

    row = lax.broadcasted_iota(jnp.int32, (RR, 1), 0) % R
    valid = jnp.logical_or(sj != n_super - 1, row >= PAD)

    r = ps[:, 0:RWKV_WIDTH]
    k = jnp.where(valid, ps[:, RWKV_WIDTH:2 * RWKV_WIDTH], 0.0)
    v = ps[:, 2 * RWKV_WIDTH:3 * RWKV_WIDTH]
    lo = ps[:, 3 * RWKV_WIDTH:]
    lo_b = lo.astype(BF16)
    ones_b = ones_ref[...]

    def segsum(x):
        return jnp.concatenate([_dot2_right(x[:, :GROUP], ones_b), _dot2_right(x[:, GROUP:], ones_b)], axis=1)

    w = w0_ref[...] + _dot(jnp.tanh(lo).astype(BF16), w1_ref[...])
    lw = jnp.where(valid, -DECAY_SCALE * _sigmoid(w), 0.0)
    a_z = _sigmoid(a0_ref[...] + _dot(lo_b, w2_ref[...]))
    kkv = k * kk_ref[...]
    kkn = kkv * jnp.minimum(lax.rsqrt(segsum(kkv * kkv)), 1e12)
    kdir = k * (1.0 + (a_z - 1.0) * ka_ref[...])

    ti = lax.broadcasted_iota(jnp.int32, (RR, RR), 0)
    si = lax.broadcasted_iota(jnp.int32, (RR, RR), 1)
    tri = jnp.where(jnp.logical_and(ti // L == si // L, (ti - si) * sign >= 0), 1.0, 0.0).astype(BF16)
    c_inc = _dot3_left(tri, lw)
    last = L - 1 if z == 0 else 0
    ctot = [c_inc[ch * L + last:ch * L + last + 1, :] for ch in range(n_ch)]
    ctot_rows = jnp.concatenate([jnp.broadcast_to(ct, (L, RWKV_WIDTH)) for ct in ctot], axis=0)
    e_neg = jnp.exp(-c_inc)
    g_rel = jnp.exp(ctot_rows - c_inc)
    rh = r * jnp.exp(c_inc)
    ah = -kkn * jnp.exp(c_inc - lw)
    kka = kkn * a_z
    bh = kka * e_neg
    kh = kdir * e_neg
    bg = kka * g_rel
    kg = kdir * g_rel

    bkt = [jnp.concatenate([bg[ch * L:(ch + 1) * L], kg[ch * L:(ch + 1) * L]], axis=0).T for ch in range(n_ch)]
    gam = [jnp.broadcast_to(jnp.exp(ct), (LANES, RWKV_WIDTH)).T for ct in ctot]

    lane_g = lax.broadcasted_iota(jnp.int32, (1, GROUP), 1)
    hm = [lane_g // HEAD_DIM == i for i in range(HEADS_PER_GROUP)]
    d4 = (lax.broadcasted_iota(jnp.int32, (L, GROUP), 0)
          - lax.broadcasted_iota(jnp.int32, (L, GROUP), 1) % L) * sign
    strict4 = d4 > 0
    incl4 = d4 >= 0
    eye4 = jnp.where(d4 == 0, 1.0, 0.0)
    bdmask = (lax.broadcasted_iota(jnp.int32, (GROUP, GROUP), 0) // HEAD_DIM
              == lax.broadcasted_iota(jnp.int32, (GROUP, GROUP), 1) // HEAD_DIM)
    zb = jnp.zeros((), BF16)

    def blocks(x4b):
        return [jnp.where(hm[i], x4b, zb) for i in range(HEADS_PER_GROUP)]

    def bd(x4b):
        return jnp.concatenate(blocks(x4b), axis=0)

    probs = [(ch, gq) for ch in range(n_ch) for gq in range(2)]

    def sl(pb):
        ch, gq = pb
        return slice(ch * L, (ch + 1) * L), slice(gq * GROUP, (gq + 1) * GROUP)

    aab, aak, arb, ark = {}, {}, {}, {}
    for pb in probs:
        rs, ls = sl(pb)
        lhs = jnp.concatenate([ah[rs, ls], rh[rs, ls]], axis=0).astype(BF16)
        rhs = jnp.concatenate(blocks(bh[rs, ls].astype(BF16)) + blocks(kh[rs, ls].astype(BF16)), axis=0)
        g = _dot_nt(lhs, rhs)
        aab[pb] = jnp.where(strict4, g[:L, :GROUP], 0.0)
        aak[pb] = jnp.where(strict4, g[:L, GROUP:], 0.0)
        arb[pb] = jnp.where(incl4, g[L:, :GROUP], 0.0)
        ark[pb] = jnp.where(incl4, g[L:, GROUP:], 0.0)

    vblk, wv, pw, tinv = {}, {}, {}, {}
    for pb in probs:
        rs, ls = sl(pb)
        vblk[pb] = blocks(v[rs, ls].astype(BF16))
        aab_b = aab[pb].astype(BF16)
        wv[pb] = _dot(aak[pb].astype(BF16), jnp.concatenate(vblk[pb], axis=0))
        pw[pb] = _dot(aab_b, bd(aab_b))
        tinv[pb] = eye4 + aab[pb]

    for _ in range(4):
        for pb in probs:
            p_b = pw[pb].astype(BF16)
            res = _dot(jnp.concatenate([tinv[pb].astype(BF16), p_b], axis=0), bd(p_b))
            tinv[pb] = tinv[pb] + res[:L]
            pw[pb] = res[L:]
    for pb in probs:
        tinv[pb] = tinv[pb] + _dot(tinv[pb].astype(BF16), bd(pw[pb].astype(BF16)))

    t_b, u_b = {}, {}
    for pb in probs:
        rs, ls = sl(pb)
        rhs = jnp.concatenate([bd(ah[rs, ls].astype(BF16)), bd(wv[pb].astype(BF16))], axis=1)
        tu = _dot(tinv[pb].astype(BF16), rhs)
        t_b[pb] = tu[:, :GROUP].astype(BF16)
        u_b[pb] = tu[:, GROUP:].astype(BF16)

    omega, oloc, pc, qq = {}, {}, {}, {}
    zero_blk = jnp.zeros((L, GROUP), BF16)
    for pb in probs:
        ch, gq = pb
        rs, ls = sl(pb)
        tblk, ublk = blocks(t_b[pb]), blocks(u_b[pb])
        omega[pb] = rh[rs, ls] + _dot(arb[pb].astype(BF16), jnp.concatenate(tblk, axis=0))
        oloc[pb] = _dot(jnp.concatenate([arb[pb], ark[pb]], axis=1).astype(BF16),
                        jnp.concatenate(ublk + vblk[pb], axis=0))
        bkt4 = jnp.concatenate(
            [bkt[ch][gq * GROUP + i * HEAD_DIM:gq * GROUP + (i + 1) * HEAD_DIM, :]
             for i in range(HEADS_PER_GROUP)], axis=1).astype(BF16)
        rhs_p, rhs_q = [], []
        for i in range(HEADS_PER_GROUP):
            rhs_p += [tblk[i], zero_blk]
            rhs_q += [ublk[i], vblk[pb][i]]
        pq = _dot(bkt4, jnp.concatenate([jnp.concatenate(rhs_p, axis=0), jnp.concatenate(rhs_q, axis=0)], axis=1))
        pc[pb] = pq[:, :GROUP]
        qq[pb] = pq[:, GROUP:]

    o4 = {}
    hbd = {(sq, gq): hs_ref[sq, gq] for sq in range(NB) for gq in range(2)}
    for local in ((0, 1) if z == 0 else (1, 0)):
        for sq in range(NB):
            for gq in range(2):
                ch = sq * (R // L) + local
                pb = (ch, gq)
                hb = hbd[(sq, gq)].astype(BF16)
                o4[pb] = _dot(omega[pb].astype(BF16), hb) + oloc[pb]
                upd = _dot(pc[pb].astype(BF16), hb) + qq[pb]
                g_rows = gam[ch][gq * GROUP:(gq + 1) * GROUP, :]
                hbd[(sq, gq)] = (jnp.concatenate([g_rows, g_rows], axis=1) * hbd[(sq, gq)]
                                 + jnp.where(bdmask, jnp.concatenate([upd] * HEADS_PER_GROUP, axis=0), 0.0))
    for (sq, gq), hval in hbd.items():
        hs_ref[sq, gq] = hval
    o = jnp.concatenate([jnp.concatenate([o4[(ch, 0)], o4[(ch, 1)]], axis=1) for ch in range(n_ch)], axis=0)

    if z == 1:
        for sq in range(NB):
            out_ref[sq] = o[sq * R:(sq + 1) * R]
    else:
        osum = o + jnp.concatenate([ob_ref[sq] for sq in range(NB)], axis=0)
        inv_hd = 1.0 / HEAD_DIM
        mo = segsum(osum) * inv_hd
        dlt = osum - mo
        vo = segsum(dlt * dlt) * inv_hd
        on = dlt * lax.rsqrt(vo + GN_EPS) * gg_ref[...] + gb_ref[...]
        a_bwd = _sigmoid(a0o_ref[...] + _dot(lo_b, w2o_ref[...]))
        kdir_bwd = k * (1.0 + (a_bwd - 1.0) * ka_ref[...])
        bonus = segsum(r * (kdir + kdir_bwd) * rk_ref[...]) * v
        gate = _dot(_sigmoid(lo).astype(BF16), w3_ref[...])
        y = (on + bonus) * gate
        for sq in range(NB):
            out_ref[sq] = y[sq * R:(sq + 1) * R]


def _rwkv_dir(pr, lp, ones256, direction, o_bwd=None):
    B, Tp, _ = pr.shape
    n_super = Tp // SUPER
    n8 = Tp // 8
    z = direction
    nb = SEQS_PER_STEP
    assert B % nb == 0

    def sblk(c):
        cs = c if z == 0 else n_super - 1 - c
        return lax.rem(cs + n_super - 1, n_super)

    def const(shape):
        return pl.BlockSpec(shape, lambda bi, c: (0,) * len(shape))

    row_spec = lambda width: pl.BlockSpec((nb, SUPER, width), lambda bi, c: (bi, sblk(c), 0))
    vec = const((1, RWKV_WIDTH))
    lora = const((LORA_WIDTH, RWKV_WIDTH))
    in_specs = [
        row_spec(RWKV_IN_WIDTH),
        pl.BlockSpec((nb, 8, RWKV_IN_WIDTH),
                     lambda bi, c: (bi, lax.rem(sblk(c) * (SUPER // 8) + n8 - 1, n8), 0)),
        pl.BlockSpec((nb, 8, RWKV_IN_WIDTH),
                     lambda bi, c: (bi, lax.rem((sblk(c) + 1) * (SUPER // 8), n8), 0)),
    ]
    args = [pr, pr, pr]
    if z == 0:
        in_specs.append(row_spec(RWKV_WIDTH))
        args.append(o_bwd)
    in_specs += [const((2, RWKV_IN_WIDTH)), vec, lora, vec, lora]
    args += [lp["mu"], lp["w0"][z], lp["w1"][z], lp["a0"][z], lp["w2"][z]]
    if z == 0:
        in_specs += [vec, lora, lora]
        args += [lp["a0"][1], lp["w2"][1], lp["w3"]]
    in_specs += [vec, vec]
    args += [lp["k_k"], lp["k_a"]]
    if z == 0:
        in_specs += [vec, vec, vec]
        args += [lp["r_k"], lp["gn_g"], lp["gn_b"]]
    in_specs.append(const((GROUP, GROUP)))
    args.append(ones256)
    return pl.pallas_call(
        functools.partial(_rwkv_kernel, direction=z, n_super=n_super),
        grid=(B // nb, n_super),
        in_specs=in_specs,
        out_specs=row_spec(RWKV_WIDTH),
        out_shape=jax.ShapeDtypeStruct((B, Tp, RWKV_WIDTH), F32),
        scratch_shapes=[
            pltpu.VMEM((nb, 2, GROUP, GROUP), F32),
            pltpu.VMEM((nb, SUPER + 16, RWKV_IN_WIDTH), F32),
        ],
        compiler_params=pltpu.CompilerParams(
            dimension_semantics=("arbitrary", "arbitrary"), vmem_limit_bytes=VMEM_LIMIT),
        name="rwkv_fwd" if z == 0 else "rwkv_bwd",
    )(*args)


def _rms_rope(x, gain, cos, sin, ones_b):
    ms = _dot3_right(x * x, ones_b) * (1.0 / HEAD_DIM)
    xn = x * lax.rsqrt(ms + QK_EPS) * gain
    lane = lax.broadcasted_iota(jnp.int32, (1, LANES), 1)
    first = (lane % (2 * ROPE_FREQS)) < ROPE_FREQS
    partner = jnp.where(first, pltpu.roll(xn, LANES - ROPE_FREQS, axis=1),
                        pltpu.roll(xn, ROPE_FREQS, axis=1))
    return xn * cos + partner * sin


def _attn_kernel(q_ref, kv_ref, cq_ref, sq_ref, ck_ref, sk_ref, qg_ref, kg_ref, ones_ref,
                 o_ref, kt_ref, vv_ref, *, n_real):
    N = n_real
    i = pl.program_id(1)
    ones_b = ones_ref[...]
    lane = lax.broadcasted_iota(jnp.int32, (1, LANES), 1)
    low = lane < HEAD_DIM

    @pl.when(i == 0)
    def _():
        def put(kn, vv, col0, width):
            kt = kn.T
            zero = jnp.zeros((HEAD_DIM, width), F32)
            cols = pl.ds(col0, width)
            kt_ref[0, :, cols] = jnp.concatenate([kt[:HEAD_DIM], zero], axis=0).astype(BF16)
            kt_ref[1, :, cols] = jnp.concatenate([zero, kt[:HEAD_DIM]], axis=0).astype(BF16)
            kt_ref[2, :, cols] = jnp.concatenate([kt[HEAD_DIM:], zero], axis=0).astype(BF16)
            kt_ref[3, :, cols] = jnp.concatenate([zero, kt[HEAD_DIM:]], axis=0).astype(BF16)
            one_hi = jnp.where(lane == HEAD_DIM, 1.0, 0.0)
            one_lo = jnp.where(lane == 0, 1.0, 0.0)
            vsw = pltpu.roll(vv, HEAD_DIM, axis=1)
            vv_ref[0, cols, :] = jnp.where(low, vv, one_hi).astype(BF16)
            vv_ref[1, cols, :] = jnp.where(low, one_lo, vsw).astype(BF16)
            vv_ref[2, cols, :] = jnp.where(low, vsw, one_hi).astype(BF16)
            vv_ref[3, cols, :] = jnp.where(low, one_lo, vv).astype(BF16)

        ck = min(K_PREP_ROWS, N)

        def body(rb, carry):
            r0 = pl.multiple_of(rb * ck, ck)
            rows = pl.ds(r0, ck)
            kvc = kv_ref[0, rows, :]
            kn = _rms_rope(kvc[:, :KV_WIDTH], kg_ref[...], ck_ref[rows, :], sk_ref[rows, :], ones_b)
            put(kn, kvc[:, KV_WIDTH:], r0, ck)
            return carry

        lax.fori_loop(0, N // ck, body, 0)
        kvt = kv_ref[0, N:N + TAIL, :]
        knt = _rms_rope(kvt[:, :KV_WIDTH], kg_ref[...], ck_ref[N:N + TAIL, :], sk_ref[N:N + TAIL, :], ones_b)
        put(knt, kvt[:, KV_WIDTH:], N, TAIL)

    tail_col = lax.broadcasted_iota(jnp.int32, (1, TAIL), 1)
    tail_bias = jnp.where(tail_col >= PAD, 0.0, NEG_BIG)

    q = q_ref[0]
    cq = cq_ref[...]
    sq = sq_ref[...]
    heads = [(jp, e) for jp in range(ATTN_WIDTH // LANES) for e in range(2)]
    qp = [(_rms_rope(q[:, jp * LANES:(jp + 1) * LANES], qg_ref[...], cq, sq, ones_b)
           * (HEAD_DIM ** -0.5 * LOG2_E)).astype(BF16) for jp in range(ATTN_WIDTH // LANES)]

    def scores(jp, e):
        kidx = 2 * ((2 * jp + e) // 4) + e
        return jnp.concatenate(
            [_dot(qp[jp], kt_ref[kidx, :, :N]), _dot(qp[jp], kt_ref[kidx, :, N:]) + tail_bias], axis=1)

    def weights(s):
        return jnp.exp2(s - jnp.max(s, axis=-1, keepdims=True)).astype(BF16)

    n_heads = len(heads)
    s = {0: scores(*heads[0]), 1: scores(*heads[1])}
    p = {0: weights(s.pop(0))}
    halves = {}
    for n, (jp, e) in enumerate(heads):
        if n + 2 < n_heads:
            s[n + 2] = scores(*heads[n + 2])
        if n + 1 < n_heads:
            p[n + 1] = weights(s.pop(n + 1))
        vidx = 2 * ((2 * jp + e) // 4) + e
        acc = _dot(p.pop(n), vv_ref[vidx])
        ones_lane = (1 - e) * HEAD_DIM
        halves[(jp, e)] = acc / acc[:, ones_lane:ones_lane + 1]
        if e == 1:
            o_ref[0, :, jp * LANES:(jp + 1) * LANES] = jnp.where(low, halves[(jp, 0)], halves[(jp, 1)])


def _attention(q, kv, cos_t, sin_t, q_gain, k_gain, ones128):
    B, Tp, _ = q.shape
    N = Tp - TAIL
    tq = min(Q_TILE, N)
    nq = N // tq
    nk = N + TAIL
    return pl.pallas_call(
        functools.partial(_attn_kernel, n_real=N),
        grid=(B, nq + 1),
        in_specs=[
            pl.BlockSpec((1, tq, ATTN_WIDTH), lambda bi, i: (bi, i, 0)),
            pl.BlockSpec((1, Tp, 2 * KV_WIDTH), lambda bi, i: (bi, 0, 0)),
            pl.BlockSpec((tq, LANES), lambda bi, i: (i, 0)),
            pl.BlockSpec((tq, LANES), lambda bi, i: (i, 0)),
            pl.BlockSpec((Tp, LANES), lambda bi, i: (0, 0)),
            pl.BlockSpec((Tp, LANES), lambda bi, i: (0, 0)),
            pl.BlockSpec((1, LANES), lambda bi, i: (0, 0)),
            pl.BlockSpec((1, LANES), lambda bi, i: (0, 0)),
            pl.BlockSpec((LANES, LANES), lambda bi, i: (0, 0)),
        ],
        out_specs=pl.BlockSpec((1, tq, ATTN_WIDTH), lambda bi, i: (bi, i, 0)),
        out_shape=jax.ShapeDtypeStruct((B, Tp, ATTN_WIDTH), F32),
        scratch_shapes=[
            pltpu.VMEM((4, LANES, nk), BF16),
            pltpu.VMEM((4, nk, LANES), BF16),
        ],
        compiler_params=pltpu.CompilerParams(
            dimension_semantics=("arbitrary", "arbitrary"), vmem_limit_bytes=VMEM_LIMIT),
        name="attention",
    )(q, kv, cos_t, sin_t, cos_t, sin_t, q_gain, k_gain, ones128)


def _post_kernel(h_ref, yr_ref, ya_ref, wor_ref, woa_ref, g1_ref, b1_ref,
                 wg_ref, wu_ref, wd_ref, g2_ref, b2_ref, o_ref):
    mix = _dot(yr_ref[...].astype(BF16), wor_ref[...]) + _dot(ya_ref[...].astype(BF16), woa_ref[...])
    x1 = _layer_norm(DEEPNORM_ALPHA * h_ref[...] + mix, g1_ref[...], b1_ref[...])
    x1b = x1.astype(BF16)
    ffn = None
    for jc in range(D_FF // FF_CHUNK):
        cols = slice(jc * FF_CHUNK, (jc + 1) * FF_CHUNK)
        gate = _dot(x1b, wg_ref[:, cols])
        up = _dot(x1b, wu_ref[:, cols])
        half = 0.5 * gate
        act = ((half + half * jnp.tanh(half)) * up).astype(BF16)
        part = _dot(act, wd_ref[cols, :])
        ffn = part if ffn is None else ffn + part
    o_ref[...] = _layer_norm(DEEPNORM_ALPHA * x1 + ffn, g2_ref[...], b2_ref[...])


def _post(h2, yr2, ya2, lp):
    M, D = h2.shape
    tm = _pick_tile(M, ROW_TILE)

    def const(shape):
        return pl.BlockSpec(shape, lambda i: (0, 0), pipeline_mode=pl.Buffered(1))

    return pl.pallas_call(
        _post_kernel,
        grid=(M // tm,),
        in_specs=[
            pl.BlockSpec((tm, D), lambda i: (i, 0)),
            pl.BlockSpec((tm, RWKV_WIDTH), lambda i: (i, 0)),
            pl.BlockSpec((tm, ATTN_WIDTH), lambda i: (i, 0)),
            const((RWKV_WIDTH, D)),
            const((ATTN_WIDTH, D)),
            const((1, D)),
            const((1, D)),
            const((D, D_FF)),
            const((D, D_FF)),
            const((D_FF, D)),
            const((1, D)),
            const((1, D)),
        ],
        out_specs=pl.BlockSpec((tm, D), lambda i: (i, 0)),
        out_shape=jax.ShapeDtypeStruct((M, D), F32),
        compiler_params=pltpu.CompilerParams(
            dimension_semantics=("arbitrary",), vmem_limit_bytes=VMEM_LIMIT),
        name="post",
    )(h2, yr2, ya2, lp["wo_r"], lp["wo_a"], lp["ln1_g"], lp["ln1_b"],
      lp["w_gate"], lp["w_up"], lp["w_down"], lp["ln2_g"], lp["ln2_b"])


def _rope_tables(n_real):
    tok = jnp.arange(n_real)
    pos = jnp.stack([tok // GRID_W, tok % GRID_W], axis=-1).astype(F32)
    inv_freq = ROPE_THETA ** (-jnp.arange(ROPE_FREQS, dtype=F32) / ROPE_FREQS)
    ang = pos[:, :, None] * inv_freq
    ang = jnp.concatenate([ang, jnp.zeros((TAIL, 2, ROPE_FREQS), F32)], axis=0)
    cos, sin = jnp.cos(ang), jnp.sin(ang)
    cos64 = jnp.concatenate([cos[:, 0], cos[:, 0], cos[:, 1], cos[:, 1]], axis=-1)
    sin64 = jnp.concatenate([-sin[:, 0], sin[:, 0], -sin[:, 1], sin[:, 1]], axis=-1)
    return jnp.tile(cos64, (1, 2)), jnp.tile(sin64, (1, 2))


def _block_ones(width):
    idx = jnp.arange(width) // HEAD_DIM
    return (idx[:, None] == idx[None, :]).astype(BF16)


def _layer_params(l, w_in, shift_mu, decay_w0, decay_up, iclr_a0, iclr_up, gate_up, k_k, k_a, r_k,
                  gn_g, gn_b, q_gain, k_gain, w_out, ln1_g, ln1_b, w_ffn_in, w_ffn_out, ln2_g, ln2_b):
    def lora(up, row0):
        rows = up.shape[-2]
        full = jnp.zeros(up.shape[:-2] + (LORA_WIDTH, RWKV_WIDTH), F32)
        return full.at[..., row0:row0 + rows, :].set(up).astype(BF16)

    row = lambda a: a.reshape(1, -1)
    return dict(
        w_in=w_in[l].astype(BF16),
        mu=shift_mu[l],
        w0=decay_w0[l].reshape(2, 1, RWKV_WIDTH),
        w1=lora(decay_up[l], 0),
        a0=iclr_a0[l].reshape(2, 1, RWKV_WIDTH),
        w2=lora(iclr_up[l], 32),
        w3=lora(gate_up[l], 64),
        k_k=row(k_k[l]), k_a=row(k_a[l]), r_k=row(r_k[l]), gn_g=row(gn_g[l]), gn_b=row(gn_b[l]),
        q_gain=jnp.tile(row(q_gain[l]), (1, 2)), k_gain=jnp.tile(row(k_gain[l]), (1, 2)),
        wo_r=w_out[l, :RWKV_WIDTH].astype(BF16), wo_a=w_out[l, RWKV_WIDTH:].astype(BF16),
        ln1_g=row(ln1_g[l]), ln1_b=row(ln1_b[l]),
        w_gate=w_ffn_in[l, :, :D_FF].astype(BF16), w_up=w_ffn_in[l, :, D_FF:].astype(BF16),
        w_down=w_ffn_out[l].astype(BF16),
        ln2_g=row(ln2_g[l]), ln2_b=row(ln2_b[l]),
    )


def _trunk(x, meta_tokens, ln_in_g, ln_in_b, layers, ones256, ones128):
    B, N, D = x.shape
    Tp = N + TAIL
    cos_t, sin_t = _rope_tables(N)
    h = _embed(x, meta_tokens, ln_in_g, ln_in_b)
    for lp in layers:
        pr, q, kv = _proj(h.reshape(B * Tp, D), lp["w_in"], Tp)
        pr = pr.reshape(B, Tp, RWKV_IN_WIDTH)
        o_bwd = _rwkv_dir(pr, lp, ones256, 1)
        y_rwkv = _rwkv_dir(pr, lp, ones256, 0, o_bwd)
        y_attn = _attention(q.reshape(B, Tp, ATTN_WIDTH), kv.reshape(B, Tp, 2 * KV_WIDTH),
                            cos_t, sin_t, lp["q_gain"], lp["k_gain"], ones128)
        h = _post(h.reshape(B * Tp, D), y_rwkv.reshape(B * Tp, RWKV_WIDTH),
                  y_attn.reshape(B * Tp, ATTN_WIDTH), lp).reshape(B, Tp, D)
    return h[:, :N]


def kernel(x_prompt, x_sample, meta_tokens, ln_in_g, ln_in_b, w_in, shift_mu, decay_w0, decay_up,
           iclr_a0, iclr_up, gate_up, k_k, k_a, r_k, gn_g, gn_b, q_gain, k_gain, w_out,
           ln1_g, ln1_b, w_ffn_in, w_ffn_out, ln2_g, ln2_b):
    layers = [
        _layer_params(l, w_in, shift_mu, decay_w0, decay_up, iclr_a0, iclr_up, gate_up, k_k, k_a, r_k,
                      gn_g, gn_b, q_gain, k_gain, w_out, ln1_g, ln1_b, w_ffn_in, w_ffn_out, ln2_g, ln2_b)
        for l in range(w_in.shape[0])
    ]
    ones256 = _block_ones(GROUP)
    ones128 = _block_ones(LANES)
    y_prompt = _trunk(x_prompt, meta_tokens, ln_in_g, ln_in_b, layers, ones256, ones128)
    y_sample = _trunk(x_sample, meta_tokens, ln_in_g, ln_in_b, layers, ones256, ones128)
    return (y_prompt, y_sample)
```

```python
import functools
import math

import jax
import jax.numpy as jnp
from jax import lax
from jax.experimental import pallas as pl
from jax.experimental.pallas import tpu as pltpu

F32 = jnp.float32
BF16 = jnp.bfloat16

D_MODEL = 1024
DEPTH = 4
HEAD_DIM = 64
RWKV_WIDTH = 512
ATTN_WIDTH = 512
KV_WIDTH = 128
LORA_WIDTH = 128
RWKV_IN_WIDTH = 3 * RWKV_WIDTH + LORA_WIDTH
IN_WIDTH = RWKV_IN_WIDTH + ATTN_WIDTH + 2 * KV_WIDTH
D_FF = 2816
FF_CHUNK = 1408
N_META = 16
GRID_W = 64
ROPE_THETA = 10000.0
ROPE_FREQS = 16
DEEPNORM_ALPHA = (2.0 * DEPTH) ** 0.25
LN_EPS = 1e-5
GN_EPS = 64e-5
QK_EPS = 1e-6
DECAY_SCALE = math.exp(-0.5)
LOG2_E = math.log2(math.e)

LANES = 128
MXU_TILE = 256
CHUNK = 64
INV_BASE = 8
SUPER = 2 * CHUNK
SEQS_PER_STEP = 2
GROUP = 4 * HEAD_DIM
HEADS_PER_GROUP = GROUP // HEAD_DIM
TAIL = 128
PAD = TAIL - N_META
ROW_TILE = 512
Q_TILE = 256
K_PREP_ROWS = 512
NEG_BIG = -1e30
VMEM_LIMIT = 56 * 1024 * 1024

assert GROUP == MXU_TILE and RWKV_WIDTH == 2 * GROUP and SUPER == LANES


def _dot(a, b):
    return jnp.dot(a, b, preferred_element_type=F32)


def _dot_nt(a, b):
    return lax.dot_general(a, b, (((1,), (1,)), ((), ())), preferred_element_type=F32)


def _split3(x):
    hi = x.astype(BF16)
    r1 = x - hi.astype(F32)
    mid = r1.astype(BF16)
    lo = (r1 - mid.astype(F32)).astype(BF16)
    return hi, mid, lo


def _dot3_right(x, m):
    hi, mid, lo = _split3(x)
    return _dot(hi, m) + _dot(mid, m) + _dot(lo, m)


def _dot2_right(x, m):
    hi = x.astype(BF16)
    lo = (x - hi.astype(F32)).astype(BF16)
    return _dot(hi, m) + _dot(lo, m)


def _dot3_left(m, x):
    hi, mid, lo = _split3(x)
    return _dot(m, hi) + _dot(m, mid) + _dot(m, lo)


def _layer_norm(x, g, b):
    mu = jnp.mean(x, axis=-1, keepdims=True)
    xc = x - mu
    var = jnp.mean(xc * xc, axis=-1, keepdims=True)
    return xc * lax.rsqrt(var + LN_EPS) * g + b


def _sigmoid(x):
    return 0.5 * jnp.tanh(0.5 * x) + 0.5


def _pick_tile(total, pref):
    t = pref
    while total % t:
        t //= 2
    return t


def _embed_kernel(x_ref, meta_ref, g_ref, b_ref, o_ref, *, n_x_tiles):
    i = pl.program_id(1)

    @pl.when(i < n_x_tiles)
    def _():
        o_ref[0] = _layer_norm(x_ref[0], g_ref[...], b_ref[...])

    @pl.when(i == n_x_tiles)
    def _():
        o_ref[0] = jnp.zeros(o_ref.shape[1:], F32)
        o_ref[0, PAD:TAIL, :] = _layer_norm(meta_ref[...], g_ref[...], b_ref[...])


def _embed(x, meta, g, b):
    B, N, D = x.shape
    te = _pick_tile(N, ROW_TILE)
    n_x_tiles = N // te
    assert te >= TAIL
    return pl.pallas_call(
        functools.partial(_embed_kernel, n_x_tiles=n_x_tiles),
        grid=(B, n_x_tiles + 1),
        in_specs=[
            pl.BlockSpec((1, te, D), lambda bi, i: (bi, jnp.minimum(i, n_x_tiles - 1), 0)),
            pl.BlockSpec((N_META, D), lambda bi, i: (0, 0)),
            pl.BlockSpec((1, D), lambda bi, i: (0, 0)),
            pl.BlockSpec((1, D), lambda bi, i: (0, 0)),
        ],
        out_specs=pl.BlockSpec((1, te, D), lambda bi, i: (bi, i, 0)),
        out_shape=jax.ShapeDtypeStruct((B, N + TAIL, D), F32),
        compiler_params=pltpu.CompilerParams(dimension_semantics=("arbitrary", "arbitrary")),
        name="embed",
    )(x, meta, g.reshape(1, D), b.reshape(1, D))


def _proj_kernel(h_ref, w_ref, pr_ref, q_ref, kv_ref, *, rows_per_seq):
    tm = h_ref.shape[0]
    y = _dot(h_ref[...].astype(BF16), w_ref[...])
    row0 = pl.program_id(0) * tm
    pad0 = lax.div(row0, rows_per_seq) * rows_per_seq + (rows_per_seq - TAIL) - row0
    row = lax.broadcasted_iota(jnp.int32, (tm, 1), 0)
    is_pad = jnp.logical_and(row >= pad0, row < pad0 + PAD)
    pr_ref[...] = jnp.where(is_pad, 0.0, y[:, :RWKV_IN_WIDTH])
    q_ref[...] = y[:, RWKV_IN_WIDTH:RWKV_IN_WIDTH + ATTN_WIDTH]
    kv_ref[...] = y[:, RWKV_IN_WIDTH + ATTN_WIDTH:]


def _proj(h2, w_in_b, rows_per_seq):
    M, D = h2.shape
    tm = _pick_tile(M, ROW_TILE)
    assert tm <= rows_per_seq - TAIL
    return pl.pallas_call(
        functools.partial(_proj_kernel, rows_per_seq=rows_per_seq),
        grid=(M // tm,),
        in_specs=[
            pl.BlockSpec((tm, D), lambda i: (i, 0)),
            pl.BlockSpec((D, IN_WIDTH), lambda i: (0, 0)),
        ],
        out_specs=[
            pl.BlockSpec((tm, RWKV_IN_WIDTH), lambda i: (i, 0)),
            pl.BlockSpec((tm, ATTN_WIDTH), lambda i: (i, 0)),
            pl.BlockSpec((tm, 2 * KV_WIDTH), lambda i: (i, 0)),
        ],
        out_shape=[
            jax.ShapeDtypeStruct((M, RWKV_IN_WIDTH), F32),
            jax.ShapeDtypeStruct((M, ATTN_WIDTH), F32),
            jax.ShapeDtypeStruct((M, 2 * KV_WIDTH), F32),
        ],
        compiler_params=pltpu.CompilerParams(
            dimension_semantics=("arbitrary",), vmem_limit_bytes=VMEM_LIMIT),
        name="proj",
    )(h2, w_in_b)


def _rwkv_kernel(*refs, direction, n_super):
    z = direction
    if z == 0:
        (ps_ref, kkn_ref, ob_ref, w0_ref, w1_ref, a0_ref, w2_ref, a0o_ref, w2o_ref,
         w3_ref, ka_ref, rk_ref, gg_ref, gb_ref, ones_ref, out_ref, hs_ref) = refs
    else:
        (pr_ref, prev_ref, next_ref, mu_ref, w0_ref, w1_ref, a0_ref, w2_ref,
         kk_ref, ka_ref, ones_ref, out_ref, ps_out_ref, kkn_out_ref, hs_ref, xs_ref) = refs
    L, R, NB = CHUNK, SUPER, SEQS_PER_STEP
    RR = NB * R
    n_ch = RR // L
    sign = 1 - 2 * z
    c = pl.program_id(1)
    cs = c if z == 0 else n_super - 1 - c
    sj = lax.rem(cs + n_super - 1, n_super)

    @pl.when(c == 0)
    def _():
        hs_ref[...] = jnp.zeros(hs_ref.shape, F32)

    if z == 1:
        for sq in range(NB):
            xs_ref[sq, 0:8, :] = prev_ref[sq]
            xs_ref[sq, 8:8 + R, :] = pr_ref[sq]
            xs_ref[sq, 8 + R:16 + R, :] = next_ref[sq]
        p = jnp.concatenate([pr_ref[sq] for sq in range(NB)], axis=0)
        prev = jnp.concatenate([xs_ref[sq, 7:7 + R, :] for sq in range(NB)], axis=0)
        nxt = jnp.concatenate([xs_ref[sq, 9:9 + R, :] for sq in range(NB)], axis=0)
        mu = mu_ref[...]
        ps = (1.0 - mu[0:1] - mu[1:2]) * p + mu[0:1] * prev + mu[1:2] * nxt
        for sq in range(NB):
            ps_out_ref[sq] = ps[sq * R:(sq + 1) * R]
    else:
        ps = jnp.concatenate([ps_ref[sq] for sq in range(NB)], axis=0)

    row = lax.broadcasted_iota(jnp.int32, (RR, 1), 0) % R
    valid = jnp.logical_or(sj != n_super - 1, row >= PAD)

    r = ps[:, 0:RWKV_WIDTH]
    k = jnp.where(valid, ps[:, RWKV_WIDTH:2 * RWKV_WIDTH], 0.0)
    v = ps[:, 2 * RWKV_WIDTH:3 * RWKV_WIDTH]
    lo = ps[:, 3 * RWKV_WIDTH:]
    lo_b = lo.astype(BF16)
    ones_b = ones_ref[...]

    def segsum(x):
        return jnp.concatenate([_dot2_right(x[:, :GROUP], ones_b), _dot2_right(x[:, GROUP:], ones_b)], axis=1)

    w = w0_ref[...] + _dot(jnp.tanh(lo).astype(BF16), w1_ref[...])
    lw = jnp.where(valid, -DECAY_SCALE * _sigmoid(w), 0.0)
    a_z = _sigmoid(a0_ref[...] + _dot(lo_b, w2_ref[...]))
    if z == 1:
        kkv = k * kk_ref[...]
        kkn = kkv * jnp.minimum(lax.rsqrt(segsum(kkv * kkv)), 1e12)
        for sq in range(NB):
            kkn_out_ref[sq] = kkn[sq * R:(sq + 1) * R]
    else:
        kkn = jnp.concatenate([kkn_ref[sq] for sq in range(NB)], axis=0)
    kdir = k * (1.0 + (a_z - 1.0) * ka_ref[...])

    ti = lax.broadcasted_iota(jnp.int32, (RR, RR), 0)
    si = lax.broadcasted_iota(jnp.int32, (RR, RR), 1)
    tri = jnp.where(jnp.logical_and(ti // L == si // L, (ti - si) * sign >= 0), 1.0, 0.0).astype(BF16)
    c_inc = _dot3_left(tri, lw)
    last = L - 1 if z == 0 else 0
    ctot = [c_inc[ch * L + last:ch * L + last + 1, :] for ch in range(n_ch)]
    ctot_rows = jnp.concatenate([jnp.broadcast_to(ct, (L, RWKV_WIDTH)) for ct in ctot], axis=0)
    e_neg = jnp.exp(-c_inc)
    g_rel = jnp.exp(ctot_rows - c_inc)
    rh = r * jnp.exp(c_inc)
    ah = -kkn * jnp.exp(c_inc - lw)
    kka = kkn * a_z
    bh = kka * e_neg
    kh = kdir * e_neg
    bg = kka * g_rel
    kg = kdir * g_rel

    bkt = [jnp.concatenate([bg[ch * L:(ch + 1) * L], kg[ch * L:(ch + 1) * L]], axis=0).T for ch in range(n_ch)]
    gam = [jnp.broadcast_to(jnp.exp(ct), (LANES, RWKV_WIDTH)).T for ct in ctot]

    lane_g = lax.broadcasted_iota(jnp.int32, (1, GROUP), 1)
    hm = [lane_g // HEAD_DIM == i for i in range(HEADS_PER_GROUP)]
    d4 = (lax.broadcasted_iota(jnp.int32, (L, GROUP), 0)
          - lax.broadcasted_iota(jnp.int32, (L, GROUP), 1) % L) * sign
    strict4 = d4 > 0
    incl4 = d4 >= 0
    eye4 = jnp.where(d4 == 0, 1.0, 0.0)
    bdmask = (lax.broadcasted_iota(jnp.int32, (GROUP, GROUP), 0) // HEAD_DIM
              == lax.broadcasted_iota(jnp.int32, (GROUP, GROUP), 1) // HEAD_DIM)
    zb = jnp.zeros((), BF16)

    def blocks(x4b):
        return [jnp.where(hm[i], x4b, zb) for i in range(HEADS_PER_GROUP)]

    def bd(x4b):
        return jnp.concatenate(blocks(x4b), axis=0)

    probs = [(ch, gq) for ch in range(n_ch) for gq in range(2)]

    def sl(pb):
        ch, gq = pb
        return slice(ch * L, (ch + 1) * L), slice(gq * GROUP, (gq + 1) * GROUP)

    aab, aak, arb, ark = {}, {}, {}, {}
    for pb in probs:
        rs, ls = sl(pb)
        lhs = jnp.concatenate([ah[rs, ls], rh[rs, ls]], axis=0).astype(BF16)
        rhs = jnp.concatenate(blocks(bh[rs, ls].astype(BF16)) + blocks(kh[rs, ls].astype(BF16)), axis=0)
        g = _dot_nt(lhs, rhs)
        aab[pb] = jnp.where(strict4, g[:L, :GROUP], 0.0)
        aak[pb] = jnp.where(strict4, g[:L, GROUP:], 0.0)
        arb[pb] = jnp.where(incl4, g[L:, :GROUP], 0.0)
        ark[pb] = jnp.where(incl4, g[L:, GROUP:], 0.0)

    t_idx = lax.broadcasted_iota(jnp.int32, (L, GROUP), 0)
    s_idx = lax.broadcasted_iota(jnp.int32, (L, GROUP), 1) % L

    def off_diagonal(b):
        return jnp.logical_and(t_idx // (2 * b) == s_idx // (2 * b), t_idx // b != s_idx // b)

    vblk, wv, pw, tinv = {}, {}, {}, {}
    for pb in probs:
        rs, ls = sl(pb)
        vblk[pb] = blocks(v[rs, ls].astype(BF16))
        d_blk = jnp.where(t_idx // INV_BASE == s_idx // INV_BASE, aab[pb], 0.0)
        d_b = d_blk.astype(BF16)
        wv[pb] = _dot(aak[pb].astype(BF16), jnp.concatenate(vblk[pb], axis=0))
        pw[pb] = _dot(d_b, bd(d_b))
        tinv[pb] = eye4 + d_blk
    for pb in probs:
        p_b = pw[pb].astype(BF16)
        res = _dot(jnp.concatenate([tinv[pb].astype(BF16), p_b], axis=0), bd(p_b))
        tinv[pb] = tinv[pb] + res[:L]
        pw[pb] = res[L:]
    for pb in probs:
        tinv[pb] = tinv[pb] + _dot(tinv[pb].astype(BF16), bd(pw[pb].astype(BF16)))
    b = INV_BASE
    while b < L:
        mask = off_diagonal(b)
        et = {pb: _dot(jnp.where(mask, aab[pb], 0.0).astype(BF16), bd(tinv[pb].astype(BF16))) for pb in probs}
        for pb in probs:
            tinv[pb] = tinv[pb] + _dot(tinv[pb].astype(BF16), bd(et[pb].astype(BF16)))
        b *= 2

    t_b, u_b = {}, {}
    for pb in probs:
        rs, ls = sl(pb)
        rhs = jnp.concatenate([bd(ah[rs, ls].astype(BF16)), bd(wv[pb].astype(BF16))], axis=1)
        tu = _dot(tinv[pb].astype(BF16), rhs)
        t_b[pb] = tu[:, :GROUP].astype(BF16)
        u_b[pb] = tu[:, GROUP:].astype(BF16)

    omega, oloc, pc, qq = {}, {}, {}, {}
    zero_blk = jnp.zeros((L, GROUP), BF16)
    for pb in probs:
        ch, gq = pb
        rs, ls = sl(pb)
        tblk, ublk = blocks(t_b[pb]), blocks(u_b[pb])
        omega[pb] = rh[rs, ls] + _dot(arb[pb].astype(BF16), jnp.concatenate(tblk, axis=0))
        oloc[pb] = _dot(jnp.concatenate([arb[pb], ark[pb]], axis=1).astype(BF16),
                        jnp.concatenate(ublk + vblk[pb], axis=0))
        bkt4 = jnp.concatenate(
            [bkt[ch][gq * GROUP + i * HEAD_DIM:gq * GROUP + (i + 1) * HEAD_DIM, :]
             for i in range(HEADS_PER_GROUP)], axis=1).astype(BF16)
        rhs_p, rhs_q = [], []
        for i in range(HEADS_PER_GROUP):
            rhs_p += [tblk[i], zero_blk]
            rhs_q += [ublk[i], vblk[pb][i]]
        pq = _dot(bkt4, jnp.concatenate([jnp.concatenate(rhs_p, axis=0), jnp.concatenate(rhs_q, axis=0)], axis=1))
        pc[pb] = pq[:, :GROUP]
        qq[pb] = pq[:, GROUP:]

    o4 = {}
    hbd = {(sq, gq): hs_ref[sq, gq] for sq in range(NB) for gq in range(2)}
    for local in ((0, 1) if z == 0 else (1, 0)):
        for sq in range(NB):
            for gq in range(2):
                ch = sq * (R // L) + local
                pb = (ch, gq)
                hb = hbd[(sq, gq)].astype(BF16)
                o4[pb] = _dot(omega[pb].astype(BF16), hb) + oloc[pb]
                upd = _dot(pc[pb].astype(BF16), hb) + qq[pb]
                g_rows = gam[ch][gq * GROUP:(gq + 1) * GROUP, :]
                hbd[(sq, gq)] = (jnp.concatenate([g_rows, g_rows], axis=1) * hbd[(sq, gq)]
                                 + jnp.where(bdmask, jnp.concatenate([upd] * HEADS_PER_GROUP, axis=0), 0.0))
    for (sq, gq), hval in hbd.items():
        hs_ref[sq, gq] = hval
    o = jnp.concatenate([jnp.concatenate([o4[(ch, 0)], o4[(ch, 1)]], axis=1) for ch in range(n_ch)], axis=0)

    if z == 1:
        for sq in range(NB):
            out_ref[sq] = o[sq * R:(sq + 1) * R]
    else:
        osum = o + jnp.concatenate([ob_ref[sq] for sq in range(NB)], axis=0)
        inv_hd = 1.0 / HEAD_DIM
        mo = segsum(osum) * inv_hd
        dlt = osum - mo
        vo = segsum(dlt * dlt) * inv_hd
        on = dlt * lax.rsqrt(vo + GN_EPS) * gg_ref[...] + gb_ref[...]
        a_bwd = _sigmoid(a0o_ref[...] + _dot(lo_b, w2o_ref[...]))
        kdir_bwd = k * (1.0 + (a_bwd - 1.0) * ka_ref[...])
        bonus = segsum(r * (kdir + kdir_bwd) * rk_ref[...]) * v
        gate = _dot(_sigmoid(lo).astype(BF16), w3_ref[...])
        y = (on + bonus) * gate
        for sq in range(NB):
            out_ref[sq] = y[sq * R:(sq + 1) * R]


def _rwkv_dir(pr, lp, ones256, direction, o_bwd=None):
    B, Tp, _ = pr.shape
    n_super = Tp // SUPER
    n8 = Tp // 8
    z = direction
    nb = SEQS_PER_STEP
    assert B % nb == 0

    def sblk(c):
        cs = c if z == 0 else n_super - 1 - c
        return lax.rem(cs + n_super - 1, n_super)

    def const(shape):
        return pl.BlockSpec(shape, lambda bi, c: (0,) * len(shape))

    row_spec = lambda width: pl.BlockSpec((nb, SUPER, width), lambda bi, c: (bi, sblk(c), 0))
    vec = const((1, RWKV_WIDTH))
    lora = const((LORA_WIDTH, RWKV_WIDTH))
    state = pltpu.VMEM((nb, 2, GROUP, GROUP), F32)
    if z == 1:
        in_specs = [
            row_spec(RWKV_IN_WIDTH),
            pl.BlockSpec((nb, 8, RWKV_IN_WIDTH),
                         lambda bi, c: (bi, lax.rem(sblk(c) * (SUPER // 8) + n8 - 1, n8), 0)),
            pl.BlockSpec((nb, 8, RWKV_IN_WIDTH),
                         lambda bi, c: (bi, lax.rem((sblk(c) + 1) * (SUPER // 8), n8), 0)),
            const((2, RWKV_IN_WIDTH)), vec, lora, vec, lora, vec, vec,
        ]
        args = [pr, pr, pr, lp["mu"], lp["w0"][z], lp["w1"][z], lp["a0"][z], lp["w2"][z], lp["k_k"], lp["k_a"]]
        out_specs = [row_spec(RWKV_WIDTH), row_spec(RWKV_IN_WIDTH), row_spec(RWKV_WIDTH)]
        out_shape = [jax.ShapeDtypeStruct((B, Tp, RWKV_WIDTH), F32),
                     jax.ShapeDtypeStruct((B, Tp, RWKV_IN_WIDTH), F32),
                     jax.ShapeDtypeStruct((B, Tp, RWKV_WIDTH), F32)]
        scratch = [state, pltpu.VMEM((nb, SUPER + 16, RWKV_IN_WIDTH), F32)]
    else:
        o_b, ps, kkn = o_bwd
        in_specs = [row_spec(RWKV_IN_WIDTH), row_spec(RWKV_WIDTH), row_spec(RWKV_WIDTH),
                    vec, lora, vec, lora, vec, lora, lora, vec, vec, vec, vec]
        args = [ps, kkn, o_b, lp["w0"][z], lp["w1"][z], lp["a0"][z], lp["w2"][z], lp["a0"][1], lp["w2"][1],
                lp["w3"], lp["k_a"], lp["r_k"], lp["gn_g"], lp["gn_b"]]
        out_specs = row_spec(RWKV_WIDTH)
        out_shape = jax.ShapeDtypeStruct((B, Tp, RWKV_WIDTH), F32)
        scratch = [state]
    in_specs.append(const((GROUP, GROUP)))
    args.append(ones256)
    return pl.pallas_call(
        functools.partial(_rwkv_kernel, direction=z, n_super=n_super),
        grid=(B // nb, n_super),
        in_specs=in_specs,
        out_specs=out_specs,
        out_shape=out_shape,
        scratch_shapes=scratch,
        compiler_params=pltpu.CompilerParams(
            dimension_semantics=("arbitrary", "arbitrary"), vmem_limit_bytes=VMEM_LIMIT),
        name="rwkv_fwd" if z == 0 else "rwkv_bwd",
    )(*args)


def _rms_rope(x, gain, cos, sin, ones_b):
    ms = _dot3_right(x * x, ones_b) * (1.0 / HEAD_DIM)
    xn = x * lax.rsqrt(ms + QK_EPS) * gain
    lane = lax.broadcasted_iota(jnp.int32, (1, LANES), 1)
    first = (lane % (2 * ROPE_FREQS)) < ROPE_FREQS
    partner = jnp.where(first, pltpu.roll(xn, LANES - ROPE_FREQS, axis=1),
                        pltpu.roll(xn, ROPE_FREQS, axis=1))
    return xn * cos + partner * sin


def _attn_kernel(q_ref, kv_ref, cq_ref, sq_ref, ck_ref, sk_ref, qg_ref, kg_ref, ones_ref,
                 o_ref, kt_ref, vv_ref, *, n_real):
    N = n_real
    i = pl.program_id(1)
    ones_b = ones_ref[...]
    lane = lax.broadcasted_iota(jnp.int32, (1, LANES), 1)
    low = lane < HEAD_DIM

    @pl.when(i == 0)
    def _():
        def put(kn, vv, col0, width):
            kt = kn.T
            zero = jnp.zeros((HEAD_DIM, width), F32)
            cols = pl.ds(col0, width)
            kt_ref[0, :, cols] = jnp.concatenate([kt[:HEAD_DIM], zero], axis=0).astype(BF16)
            kt_ref[1, :, cols] = jnp.concatenate([zero, kt[:HEAD_DIM]], axis=0).astype(BF16)
            kt_ref[2, :, cols] = jnp.concatenate([kt[HEAD_DIM:], zero], axis=0).astype(BF16)
            kt_ref[3, :, cols] = jnp.concatenate([zero, kt[HEAD_DIM:]], axis=0).astype(BF16)
            one_hi = jnp.where(lane == HEAD_DIM, 1.0, 0.0)
            one_lo = jnp.where(lane == 0, 1.0, 0.0)
            vsw = pltpu.roll(vv, HEAD_DIM, axis=1)
            vv_ref[0, cols, :] = jnp.where(low, vv, one_hi).astype(BF16)
            vv_ref[1, cols, :] = jnp.where(low, one_lo, vsw).astype(BF16)
            vv_ref[2, cols, :] = jnp.where(low, vsw, one_hi).astype(BF16)
            vv_ref[3, cols, :] = jnp.where(low, one_lo, vv).astype(BF16)

        ck = min(K_PREP_ROWS, N)

        def body(rb, carry):
            r0 = pl.multiple_of(rb * ck, ck)
            rows = pl.ds(r0, ck)
            kvc = kv_ref[0, rows, :]
            kn = _rms_rope(kvc[:, :KV_WIDTH], kg_ref[...], ck_ref[rows, :], sk_ref[rows, :], ones_b)
            put(kn, kvc[:, KV_WIDTH:], r0, ck)
            return carry

        lax.fori_loop(0, N // ck, body, 0)
        kvt = kv_ref[0, N:N + TAIL, :]
        knt = _rms_rope(kvt[:, :KV_WIDTH], kg_ref[...], ck_ref[N:N + TAIL, :], sk_ref[N:N + TAIL, :], ones_b)
        put(knt, kvt[:, KV_WIDTH:], N, TAIL)

    tail_col = lax.broadcasted_iota(jnp.int32, (1, TAIL), 1)
    tail_bias = jnp.where(tail_col >= PAD, 0.0, NEG_BIG)

    q = q_ref[0]
    cq = cq_ref[...]
    sq = sq_ref[...]
    heads = [(jp, e) for jp in range(ATTN_WIDTH // LANES) for e in range(2)]
    qp = [(_rms_rope(q[:, jp * LANES:(jp + 1) * LANES], qg_ref[...], cq, sq, ones_b)
           * (HEAD_DIM ** -0.5 * LOG2_E)).astype(BF16) for jp in range(ATTN_WIDTH // LANES)]

    def scores(jp, e):
        kidx = 2 * ((2 * jp + e) // 4) + e
        return jnp.concatenate(
            [_dot(qp[jp], kt_ref[kidx, :, :N]), _dot(qp[jp], kt_ref[kidx, :, N:]) + tail_bias], axis=1)

    def weights(s):
        return jnp.exp2(s - jnp.max(s, axis=-1, keepdims=True)).astype(BF16)

    n_heads = len(heads)
    s = {0: scores(*heads[0]), 1: scores(*heads[1])}
    p = {0: weights(s.pop(0))}
    halves = {}
    for n, (jp, e) in enumerate(heads):
        if n + 2 < n_heads:
            s[n + 2] = scores(*heads[n + 2])
        if n + 1 < n_heads:
            p[n + 1] = weights(s.pop(n + 1))
        vidx = 2 * ((2 * jp + e) // 4) + e
        acc = _dot(p.pop(n), vv_ref[vidx])
        ones_lane = (1 - e) * HEAD_DIM
        halves[(jp, e)] = acc / acc[:, ones_lane:ones_lane + 1]
        if e == 1:
            o_ref[0, :, jp * LANES:(jp + 1) * LANES] = jnp.where(low, halves[(jp, 0)], halves[(jp, 1)])


def _attention(q, kv, cos_t, sin_t, q_gain, k_gain, ones128):
    B, Tp, _ = q.shape
    N = Tp - TAIL
    tq = min(Q_TILE, N)
    nq = N // tq
    nk = N + TAIL
    return pl.pallas_call(
        functools.partial(_attn_kernel, n_real=N),
        grid=(B, nq + 1),
        in_specs=[
            pl.BlockSpec((1, tq, ATTN_WIDTH), lambda bi, i: (bi, i, 0)),
            pl.BlockSpec((1, Tp, 2 * KV_WIDTH), lambda bi, i: (bi, 0, 0)),
            pl.BlockSpec((tq, LANES), lambda bi, i: (i, 0)),
            pl.BlockSpec((tq, LANES), lambda bi, i: (i, 0)),
            pl.BlockSpec((Tp, LANES), lambda bi, i: (0, 0)),
            pl.BlockSpec((Tp, LANES), lambda bi, i: (0, 0)),
            pl.BlockSpec((1, LANES), lambda bi, i: (0, 0)),
            pl.BlockSpec((1, LANES), lambda bi, i: (0, 0)),
            pl.BlockSpec((LANES, LANES), lambda bi, i: (0, 0)),
        ],
        out_specs=pl.BlockSpec((1, tq, ATTN_WIDTH), lambda bi, i: (bi, i, 0)),
        out_shape=jax.ShapeDtypeStruct((B, Tp, ATTN_WIDTH), F32),
        scratch_shapes=[
            pltpu.VMEM((4, LANES, nk), BF16),
            pltpu.VMEM((4, nk, LANES), BF16),
        ],
        compiler_params=pltpu.CompilerParams(
            dimension_semantics=("arbitrary", "arbitrary"), vmem_limit_bytes=VMEM_LIMIT),
        name="attention",
    )(q, kv, cos_t, sin_t, cos_t, sin_t, q_gain, k_gain, ones128)


def _post_kernel(h_ref, yr_ref, ya_ref, wor_ref, woa_ref, g1_ref, b1_ref,
                 wg_ref, wu_ref, wd_ref, g2_ref, b2_ref, o_ref):
    mix = _dot(yr_ref[...].astype(BF16), wor_ref[...]) + _dot(ya_ref[...].astype(BF16), woa_ref[...])
    x1 = _layer_norm(DEEPNORM_ALPHA * h_ref[...] + mix, g1_ref[...], b1_ref[...])
    x1b = x1.astype(BF16)
    ffn = None
    for jc in range(D_FF // FF_CHUNK):
        cols = slice(jc * FF_CHUNK, (jc + 1) * FF_CHUNK)
        gate = _dot(x1b, wg_ref[:, cols])
        up = _dot(x1b, wu_ref[:, cols])
        half = 0.5 * gate
        act = ((half + half * jnp.tanh(half)) * up).astype(BF16)
        part = _dot(act, wd_ref[cols, :])
        ffn = part if ffn is None else ffn + part
    o_ref[...] = _layer_norm(DEEPNORM_ALPHA * x1 + ffn, g2_ref[...], b2_ref[...])


def _post(h2, yr2, ya2, lp):
    M, D = h2.shape
    tm = _pick_tile(M, ROW_TILE)

    def const(shape):
        return pl.BlockSpec(shape, lambda i: (0, 0), pipeline_mode=pl.Buffered(1))

    return pl.pallas_call(
        _post_kernel,
        grid=(M // tm,),
        in_specs=[
            pl.BlockSpec((tm, D), lambda i: (i, 0)),
            pl.BlockSpec((tm, RWKV_WIDTH), lambda i: (i, 0)),
            pl.BlockSpec((tm, ATTN_WIDTH), lambda i: (i, 0)),
            const((RWKV_WIDTH, D)),
            const((ATTN_WIDTH, D)),
            const((1, D)),
            const((1, D)),
            const((D, D_FF)),
            const((D, D_FF)),
            const((D_FF, D)),
            const((1, D)),
            const((1, D)),
        ],
        out_specs=pl.BlockSpec((tm, D), lambda i: (i, 0)),
        out_shape=jax.ShapeDtypeStruct((M, D), F32),
        compiler_params=pltpu.CompilerParams(
            dimension_semantics=("arbitrary",), vmem_limit_bytes=VMEM_LIMIT),
        name="post",
    )(h2, yr2, ya2, lp["wo_r"], lp["wo_a"], lp["ln1_g"], lp["ln1_b"],
      lp["w_gate"], lp["w_up"], lp["w_down"], lp["ln2_g"], lp["ln2_b"])


def _rope_tables(n_real):
    tok = jnp.arange(n_real)
    pos = jnp.stack([tok // GRID_W, tok % GRID_W], axis=-1).astype(F32)
    inv_freq = ROPE_THETA ** (-jnp.arange(ROPE_FREQS, dtype=F32) / ROPE_FREQS)
    ang = pos[:, :, None] * inv_freq
    ang = jnp.concatenate([ang, jnp.zeros((TAIL, 2, ROPE_FREQS), F32)], axis=0)
    cos, sin = jnp.cos(ang), jnp.sin(ang)
    cos64 = jnp.concatenate([cos[:, 0], cos[:, 0], cos[:, 1], cos[:, 1]], axis=-1)
    sin64 = jnp.concatenate([-sin[:, 0], sin[:, 0], -sin[:, 1], sin[:, 1]], axis=-1)
    return jnp.tile(cos64, (1, 2)), jnp.tile(sin64, (1, 2))


def _block_ones(width):
    idx = jnp.arange(width) // HEAD_DIM
    return (idx[:, None] == idx[None, :]).astype(BF16)


def _layer_params(l, w_in, shift_mu, decay_w0, decay_up, iclr_a0, iclr_up, gate_up, k_k, k_a, r_k,
                  gn_g, gn_b, q_gain, k_gain, w_out, ln1_g, ln1_b, w_ffn_in, w_ffn_out, ln2_g, ln2_b):
    def lora(up, row0):
        rows = up.shape[-2]
        full = jnp.zeros(up.shape[:-2] + (LORA_WIDTH, RWKV_WIDTH), F32)
        return full.at[..., row0:row0 + rows, :].set(up).astype(BF16)

    row = lambda a: a.reshape(1, -1)
    return dict(
        w_in=w_in[l].astype(BF16),
        mu=shift_mu[l],
        w0=decay_w0[l].reshape(2, 1, RWKV_WIDTH),
        w1=lora(decay_up[l], 0),
        a0=iclr_a0[l].reshape(2, 1, RWKV_WIDTH),
        w2=lora(iclr_up[l], 32),
        w3=lora(gate_up[l], 64),
        k_k=row(k_k[l]), k_a=row(k_a[l]), r_k=row(r_k[l]), gn_g=row(gn_g[l]), gn_b=row(gn_b[l]),
        q_gain=jnp.tile(row(q_gain[l]), (1, 2)), k_gain=jnp.tile(row(k_gain[l]), (1, 2)),
        wo_r=w_out[l, :RWKV_WIDTH].astype(BF16), wo_a=w_out[l, RWKV_WIDTH:].astype(BF16),
        ln1_g=row(ln1_g[l]), ln1_b=row(ln1_b[l]),
        w_gate=w_ffn_in[l, :, :D_FF].astype(BF16), w_up=w_ffn_in[l, :, D_FF:].astype(BF16),
        w_down=w_ffn_out[l].astype(BF16),
        ln2_g=row(ln2_g[l]), ln2_b=row(ln2_b[l]),
    )


def _trunk(x, meta_tokens, ln_in_g, ln_in_b, layers, ones256, ones128):
    B, N, D = x.shape
    Tp = N + TAIL
    cos_t, sin_t = _rope_tables(N)
    h = _embed(x, meta_tokens, ln_in_g, ln_in_b)
    for lp in layers:
        pr, q, kv = _proj(h.reshape(B * Tp, D), lp["w_in"], Tp)
        pr = pr.reshape(B, Tp, RWKV_IN_WIDTH)
        o_bwd = _rwkv_dir(pr, lp, ones256, 1)
        y_rwkv = _rwkv_dir(pr, lp, ones256, 0, o_bwd)
        y_attn = _attention(q.reshape(B, Tp, ATTN_WIDTH), kv.reshape(B, Tp, 2 * KV_WIDTH),
                            cos_t, sin_t, lp["q_gain"], lp["k_gain"], ones128)
        h = _post(h.reshape(B * Tp, D), y_rwkv.reshape(B * Tp, RWKV_WIDTH),
                  y_attn.reshape(B * Tp, ATTN_WIDTH), lp).reshape(B, Tp, D)
    return h[:, :N]


def kernel(x_prompt, x_sample, meta_tokens, ln_in_g, ln_in_b, w_in, shift_mu, decay_w0, decay_up,
           iclr_a0, iclr_up, gate_up, k_k, k_a, r_k, gn_g, gn_b, q_gain, k_gain, w_out,
           ln1_g, ln1_b, w_ffn_in, w_ffn_out, ln2_g, ln2_b):
    layers = [
        _layer_params(l, w_in, shift_mu, decay_w0, decay_up, iclr_a0, iclr_up, gate_up, k_k, k_a, r_k,
                      gn_g, gn_b, q_gain, k_gain, w_out, ln1_g, ln1_b, w_ffn_in, w_ffn_out, ln2_g, ln2_b)
        for l in range(w_in.shape[0])
    ]
    ones256 = _block_ones(GROUP)
    ones128 = _block_ones(LANES)
    y_prompt = _trunk(x_prompt, meta_tokens, ln_in_g, ln_in_b, layers, ones256, ones128)
    y_sample = _trunk(x_sample, meta_tokens, ln_in_g, ln_in_b, layers, ones256, ones128)
    return (y_prompt, y_sample)
```

```python
import functools
import math

import jax
import jax.numpy as jnp
from jax import lax
from jax.experimental import pallas as pl
from jax.experimental.pallas import tpu as pltpu

F32 = jnp.float32
BF16 = jnp.bfloat16

D_MODEL = 1024
DEPTH = 4
HEAD_DIM = 64
RWKV_WIDTH = 512
ATTN_WIDTH = 512
KV_WIDTH = 128
LORA_WIDTH = 128
RWKV_IN_WIDTH = 3 * RWKV_WIDTH + LORA_WIDTH
IN_WIDTH = RWKV_IN_WIDTH + ATTN_WIDTH + 2 * KV_WIDTH
D_FF = 2816
FF_CHUNK = 1408
N_META = 16
GRID_W = 64
ROPE_THETA = 10000.0
ROPE_FREQS = 16
DEEPNORM_ALPHA = (2.0 * DEPTH) ** 0.25
LN_EPS = 1e-5
GN_EPS = 64e-5
QK_EPS = 1e-6
DECAY_SCALE = math.exp(-0.5)
LOG2_E = math.log2(math.e)

LANES = 128
MXU_TILE = 256
CHUNK = 64
INV_BASE = 8
SUPER = 2 * CHUNK
SEQS_PER_STEP = 4
GROUP = 4 * HEAD_DIM
HEADS_PER_GROUP = GROUP // HEAD_DIM
TAIL = 128
PAD = TAIL - N_META
ROW_TILE = 512
Q_TILE = 256
K_PREP_ROWS = 512
NEG_BIG = -1e30
VMEM_LIMIT = 56 * 1024 * 1024

assert GROUP == MXU_TILE and RWKV_WIDTH == 2 * GROUP and SUPER == LANES


def _dot(a, b):
    return jnp.dot(a, b, preferred_element_type=F32)


def _dot_nt(a, b):
    return lax.dot_general(a, b, (((1,), (1,)), ((), ())), preferred_element_type=F32)


def _split3(x):
    hi = x.astype(BF16)
    r1 = x - hi.astype(F32)
    mid = r1.astype(BF16)
    lo = (r1 - mid.astype(F32)).astype(BF16)
    return hi, mid, lo


def _dot3_right(x, m):
    hi, mid, lo = _split3(x)
    return _dot(hi, m) + _dot(mid, m) + _dot(lo, m)


def _dot2_right(x, m):
    hi = x.astype(BF16)
    lo = (x - hi.astype(F32)).astype(BF16)
    return _dot(hi, m) + _dot(lo, m)


def _dot3_left(m, x):
    hi, mid, lo = _split3(x)
    return _dot(m, hi) + _dot(m, mid) + _dot(m, lo)


def _layer_norm(x, g, b):
    mu = jnp.mean(x, axis=-1, keepdims=True)
    xc = x - mu
    var = jnp.mean(xc * xc, axis=-1, keepdims=True)
    return xc * lax.rsqrt(var + LN_EPS) * g + b


def _sigmoid(x):
    return 0.5 * jnp.tanh(0.5 * x) + 0.5


def _pick_tile(total, pref):
    t = pref
    while total % t:
        t //= 2
    return t


def _embed_kernel(x_ref, meta_ref, g_ref, b_ref, o_ref, *, n_x_tiles):
    i = pl.program_id(1)

    @pl.when(i < n_x_tiles)
    def _():
        o_ref[0] = _layer_norm(x_ref[0], g_ref[...], b_ref[...])

    @pl.when(i == n_x_tiles)
    def _():
        o_ref[0] = jnp.zeros(o_ref.shape[1:], F32)
        o_ref[0, PAD:TAIL, :] = _layer_norm(meta_ref[...], g_ref[...], b_ref[...])


def _embed(x, meta, g, b):
    B, N, D = x.shape
    te = _pick_tile(N, ROW_TILE)
    n_x_tiles = N // te
    assert te >= TAIL
    return pl.pallas_call(
        functools.partial(_embed_kernel, n_x_tiles=n_x_tiles),
        grid=(B, n_x_tiles + 1),
        in_specs=[
            pl.BlockSpec((1, te, D), lambda bi, i: (bi, jnp.minimum(i, n_x_tiles - 1), 0)),
            pl.BlockSpec((N_META, D), lambda bi, i: (0, 0)),
            pl.BlockSpec((1, D), lambda bi, i: (0, 0)),
            pl.BlockSpec((1, D), lambda bi, i: (0, 0)),
        ],
        out_specs=pl.BlockSpec((1, te, D), lambda bi, i: (bi, i, 0)),
        out_shape=jax.ShapeDtypeStruct((B, N + TAIL, D), F32),
        compiler_params=pltpu.CompilerParams(dimension_semantics=("arbitrary", "arbitrary")),
        name="embed",
    )(x, meta, g.reshape(1, D), b.reshape(1, D))


def _proj_kernel(h_ref, w_ref, pr_ref, q_ref, kv_ref, *, rows_per_seq):
    tm = h_ref.shape[0]
    y = _dot(h_ref[...].astype(BF16), w_ref[...])
    row0 = pl.program_id(0) * tm
    pad0 = lax.div(row0, rows_per_seq) * rows_per_seq + (rows_per_seq - TAIL) - row0
    row = lax.broadcasted_iota(jnp.int32, (tm, 1), 0)
    is_pad = jnp.logical_and(row >= pad0, row < pad0 + PAD)
    pr_ref[...] = jnp.where(is_pad, 0.0, y[:, :RWKV_IN_WIDTH])
    q_ref[...] = y[:, RWKV_IN_WIDTH:RWKV_IN_WIDTH + ATTN_WIDTH]
    kv_ref[...] = y[:, RWKV_IN_WIDTH + ATTN_WIDTH:]


def _proj(h2, w_in_b, rows_per_seq):
    M, D = h2.shape
    tm = _pick_tile(M, ROW_TILE)
    assert tm <= rows_per_seq - TAIL
    return pl.pallas_call(
        functools.partial(_proj_kernel, rows_per_seq=rows_per_seq),
        grid=(M // tm,),
        in_specs=[
            pl.BlockSpec((tm, D), lambda i: (i, 0)),
            pl.BlockSpec((D, IN_WIDTH), lambda i: (0, 0)),
        ],
        out_specs=[
            pl.BlockSpec((tm, RWKV_IN_WIDTH), lambda i: (i, 0)),
            pl.BlockSpec((tm, ATTN_WIDTH), lambda i: (i, 0)),
            pl.BlockSpec((tm, 2 * KV_WIDTH), lambda i: (i, 0)),
        ],
        out_shape=[
            jax.ShapeDtypeStruct((M, RWKV_IN_WIDTH), F32),
            jax.ShapeDtypeStruct((M, ATTN_WIDTH), F32),
            jax.ShapeDtypeStruct((M, 2 * KV_WIDTH), F32),
        ],
        compiler_params=pltpu.CompilerParams(
            dimension_semantics=("arbitrary",), vmem_limit_bytes=VMEM_LIMIT),
        name="proj",
    )(h2, w_in_b)


def _rwkv_kernel(*refs, direction, n_super):
    z = direction
    if z == 0:
        (ps_ref, kkn_ref, ob_ref, w0_ref, w1_ref, a0_ref, w2_ref, a0o_ref, w2o_ref,
         w3_ref, ka_ref, rk_ref, gg_ref, gb_ref, ones_ref, out_ref, hs_ref) = refs
    else:
        (pr_ref, prev_ref, next_ref, mu_ref, w0_ref, w1_ref, a0_ref, w2_ref,
         kk_ref, ka_ref, ones_ref, out_ref, ps_out_ref, kkn_out_ref, hs_ref, xs_ref) = refs
    L, R, NB = CHUNK, SUPER, SEQS_PER_STEP
    RR = NB * R
    n_ch = RR // L
    sign = 1 - 2 * z
    c = pl.program_id(1)
    cs = c if z == 0 else n_super - 1 - c
    sj = lax.rem(cs + n_super - 1, n_super)

    @pl.when(c == 0)
    def _():
        hs_ref[...] = jnp.zeros(hs_ref.shape, F32)

    if z == 1:
        for sq in range(NB):
            xs_ref[sq, 0:8, :] = prev_ref[sq]
            xs_ref[sq, 8:8 + R, :] = pr_ref[sq]
            xs_ref[sq, 8 + R:16 + R, :] = next_ref[sq]
        p = jnp.concatenate([pr_ref[sq] for sq in range(NB)], axis=0)
        prev = jnp.concatenate([xs_ref[sq, 7:7 + R, :] for sq in range(NB)], axis=0)
        nxt = jnp.concatenate([xs_ref[sq, 9:9 + R, :] for sq in range(NB)], axis=0)
        mu = mu_ref[...]
        ps = (1.0 - mu[0:1] - mu[1:2]) * p + mu[0:1] * prev + mu[1:2] * nxt
        for sq in range(NB):
            ps_out_ref[sq] = ps[sq * R:(sq + 1) * R]
    else:
        ps = jnp.concatenate([ps_ref[sq] for sq in range(NB)], axis=0)

    row = lax.broadcasted_iota(jnp.int32, (RR, 1), 0) % R
    valid = jnp.logical_or(sj != n_super - 1, row >= PAD)

    r = ps[:, 0:RWKV_WIDTH]
    k = jnp.where(valid, ps[:, RWKV_WIDTH:2 * RWKV_WIDTH], 0.0)
    v = ps[:, 2 * RWKV_WIDTH:3 * RWKV_WIDTH]
    lo = ps[:, 3 * RWKV_WIDTH:]
    lo_b = lo.astype(BF16)
    ones_b = ones_ref[...]

    def segsum(x):
        return jnp.concatenate([_dot2_right(x[:, :GROUP], ones_b), _dot2_right(x[:, GROUP:], ones_b)], axis=1)

    w = w0_ref[...] + _dot(jnp.tanh(lo).astype(BF16), w1_ref[...])
    lw = jnp.where(valid, -DECAY_SCALE * _sigmoid(w), 0.0)
    a_z = _sigmoid(a0_ref[...] + _dot(lo_b, w2_ref[...]))
    if z == 1:
        kkv = k * kk_ref[...]
        kkn = kkv * jnp.minimum(lax.rsqrt(segsum(kkv * kkv)), 1e12)
        for sq in range(NB):
            kkn_out_ref[sq] = kkn[sq * R:(sq + 1) * R]
    else:
        kkn = jnp.concatenate([kkn_ref[sq] for sq in range(NB)], axis=0)
    kdir = k * (1.0 + (a_z - 1.0) * ka_ref[...])

    ti = lax.broadcasted_iota(jnp.int32, (RR, RR), 0)
    si = lax.broadcasted_iota(jnp.int32, (RR, RR), 1)
    tri = jnp.where(jnp.logical_and(ti // L == si // L, (ti - si) * sign >= 0), 1.0, 0.0).astype(BF16)
    c_inc = _dot3_left(tri, lw)
    last = L - 1 if z == 0 else 0
    ctot = [c_inc[ch * L + last:ch * L + last + 1, :] for ch in range(n_ch)]
    ctot_rows = jnp.concatenate([jnp.broadcast_to(ct, (L, RWKV_WIDTH)) for ct in ctot], axis=0)
    e_neg = jnp.exp(-c_inc)
    g_rel = jnp.exp(ctot_rows - c_inc)
    rh = r * jnp.exp(c_inc)
    ah = -kkn * jnp.exp(c_inc - lw)
    kka = kkn * a_z
    bh = kka * e_neg
    kh = kdir * e_neg
    bg = kka * g_rel
    kg = kdir * g_rel

    bkt = [jnp.concatenate([bg[ch * L:(ch + 1) * L], kg[ch * L:(ch + 1) * L]], axis=0).T for ch in range(n_ch)]
    gam = [jnp.broadcast_to(jnp.exp(ct), (LANES, RWKV_WIDTH)).T for ct in ctot]

    lane_g = lax.broadcasted_iota(jnp.int32, (1, GROUP), 1)
    hm = [lane_g // HEAD_DIM == i for i in range(HEADS_PER_GROUP)]
    d4 = (lax.broadcasted_iota(jnp.int32, (L, GROUP), 0)
          - lax.broadcasted_iota(jnp.int32, (L, GROUP), 1) % L) * sign
    strict4 = d4 > 0
    incl4 = d4 >= 0
    eye4 = jnp.where(d4 == 0, 1.0, 0.0)
    bdmask = (lax.broadcasted_iota(jnp.int32, (GROUP, GROUP), 0) // HEAD_DIM
              == lax.broadcasted_iota(jnp.int32, (GROUP, GROUP), 1) // HEAD_DIM)
    zb = jnp.zeros((), BF16)

    def blocks(x4b):
        return [jnp.where(hm[i], x4b, zb) for i in range(HEADS_PER_GROUP)]

    def bd(x4b):
        return jnp.concatenate(blocks(x4b), axis=0)

    probs = [(ch, gq) for ch in range(n_ch) for gq in range(2)]

    def sl(pb):
        ch, gq = pb
        return slice(ch * L, (ch + 1) * L), slice(gq * GROUP, (gq + 1) * GROUP)

    aab, aak, arb, ark = {}, {}, {}, {}
    for pb in probs:
        rs, ls = sl(pb)
        lhs = jnp.concatenate([ah[rs, ls], rh[rs, ls]], axis=0).astype(BF16)
        rhs = jnp.concatenate(blocks(bh[rs, ls].astype(BF16)) + blocks(kh[rs, ls].astype(BF16)), axis=0)
        g = _dot_nt(lhs, rhs)
        aab[pb] = jnp.where(strict4, g[:L, :GROUP], 0.0)
        aak[pb] = jnp.where(strict4, g[:L, GROUP:], 0.0)
        arb[pb] = jnp.where(incl4, g[L:, :GROUP], 0.0)
        ark[pb] = jnp.where(incl4, g[L:, GROUP:], 0.0)

    t_idx = lax.broadcasted_iota(jnp.int32, (L, GROUP), 0)
    s_idx = lax.broadcasted_iota(jnp.int32, (L, GROUP), 1) % L

    def off_diagonal(b):
        return jnp.logical_and(t_idx // (2 * b) == s_idx // (2 * b), t_idx // b != s_idx // b)

    vblk, wv, pw, tinv = {}, {}, {}, {}
    for pb in probs:
        rs, ls = sl(pb)
        vblk[pb] = blocks(v[rs, ls].astype(BF16))
        d_blk = jnp.where(t_idx // INV_BASE == s_idx // INV_BASE, aab[pb], 0.0)
        d_b = d_blk.astype(BF16)
        wv[pb] = _dot(aak[pb].astype(BF16), jnp.concatenate(vblk[pb], axis=0))
        pw[pb] = _dot(d_b, bd(d_b))
        tinv[pb] = eye4 + d_blk
    for pb in probs:
        p_b = pw[pb].astype(BF16)
        res = _dot(jnp.concatenate([tinv[pb].astype(BF16), p_b], axis=0), bd(p_b))
        tinv[pb] = tinv[pb] + res[:L]
        pw[pb] = res[L:]
    for pb in probs:
        tinv[pb] = tinv[pb] + _dot(tinv[pb].astype(BF16), bd(pw[pb].astype(BF16)))
    b = INV_BASE
    while b < L:
        mask = off_diagonal(b)
        et = {pb: _dot(jnp.where(mask, aab[pb], 0.0).astype(BF16), bd(tinv[pb].astype(BF16))) for pb in probs}
        for pb in probs:
            tinv[pb] = tinv[pb] + _dot(tinv[pb].astype(BF16), bd(et[pb].astype(BF16)))
        b *= 2

    t_b, u_b = {}, {}
    for pb in probs:
        rs, ls = sl(pb)
        rhs = jnp.concatenate([bd(ah[rs, ls].astype(BF16)), bd(wv[pb].astype(BF16))], axis=1)
        tu = _dot(tinv[pb].astype(BF16), rhs)
        t_b[pb] = tu[:, :GROUP].astype(BF16)
        u_b[pb] = tu[:, GROUP:].astype(BF16)

    omega, oloc, pc, qq = {}, {}, {}, {}
    zero_blk = jnp.zeros((L, GROUP), BF16)
    for pb in probs:
        ch, gq = pb
        rs, ls = sl(pb)
        tblk, ublk = blocks(t_b[pb]), blocks(u_b[pb])
        omega[pb] = rh[rs, ls] + _dot(arb[pb].astype(BF16), jnp.concatenate(tblk, axis=0))
        oloc[pb] = _dot(jnp.concatenate([arb[pb], ark[pb]], axis=1).astype(BF16),
                        jnp.concatenate(ublk + vblk[pb], axis=0))
        bkt4 = jnp.concatenate(
            [bkt[ch][gq * GROUP + i * HEAD_DIM:gq * GROUP + (i + 1) * HEAD_DIM, :]
             for i in range(HEADS_PER_GROUP)], axis=1).astype(BF16)
        rhs_p, rhs_q = [], []
        for i in range(HEADS_PER_GROUP):
            rhs_p += [tblk[i], zero_blk]
            rhs_q += [ublk[i], vblk[pb][i]]
        pq = _dot(bkt4, jnp.concatenate([jnp.concatenate(rhs_p, axis=0), jnp.concatenate(rhs_q, axis=0)], axis=1))
        pc[pb] = pq[:, :GROUP]
        qq[pb] = pq[:, GROUP:]

    o4 = {}
    hbd = {(sq, gq): hs_ref[sq, gq] for sq in range(NB) for gq in range(2)}
    for local in ((0, 1) if z == 0 else (1, 0)):
        for sq in range(NB):
            for gq in range(2):
                ch = sq * (R // L) + local
                pb = (ch, gq)
                hb = hbd[(sq, gq)].astype(BF16)
                o4[pb] = _dot(omega[pb].astype(BF16), hb) + oloc[pb]
                upd = _dot(pc[pb].astype(BF16), hb) + qq[pb]
                g_rows = gam[ch][gq * GROUP:(gq + 1) * GROUP, :]
                hbd[(sq, gq)] = (jnp.concatenate([g_rows, g_rows], axis=1) * hbd[(sq, gq)]
                                 + jnp.where(bdmask, jnp.concatenate([upd] * HEADS_PER_GROUP, axis=0), 0.0))
    for (sq, gq), hval in hbd.items():
        hs_ref[sq, gq] = hval
    o = jnp.concatenate([jnp.concatenate([o4[(ch, 0)], o4[(ch, 1)]], axis=1) for ch in range(n_ch)], axis=0)

    if z == 1:
        for sq in range(NB):
            out_ref[sq] = o[sq * R:(sq + 1) * R]
    else:
        osum = o + jnp.concatenate([ob_ref[sq] for sq in range(NB)], axis=0)
        inv_hd = 1.0 / HEAD_DIM
        mo = segsum(osum) * inv_hd
        dlt = osum - mo
        vo = segsum(dlt * dlt) * inv_hd
        on = dlt * lax.rsqrt(vo + GN_EPS) * gg_ref[...] + gb_ref[...]
        a_bwd = _sigmoid(a0o_ref[...] + _dot(lo_b, w2o_ref[...]))
        kdir_bwd = k * (1.0 + (a_bwd - 1.0) * ka_ref[...])
        bonus = segsum(r * (kdir + kdir_bwd) * rk_ref[...]) * v
        gate = _dot(_sigmoid(lo).astype(BF16), w3_ref[...])
        y = (on + bonus) * gate
        for sq in range(NB):
            out_ref[sq] = y[sq * R:(sq + 1) * R]


def _rwkv_dir(pr, lp, ones256, direction, o_bwd=None):
    B, Tp, _ = pr.shape
    n_super = Tp // SUPER
    n8 = Tp // 8
    z = direction
    nb = SEQS_PER_STEP
    assert B % nb == 0

    def sblk(c):
        cs = c if z == 0 else n_super - 1 - c
        return lax.rem(cs + n_super - 1, n_super)

    def const(shape):
        return pl.BlockSpec(shape, lambda bi, c: (0,) * len(shape))

    row_spec = lambda width: pl.BlockSpec((nb, SUPER, width), lambda bi, c: (bi, sblk(c), 0))
    vec = const((1, RWKV_WIDTH))
    lora = const((LORA_WIDTH, RWKV_WIDTH))
    state = pltpu.VMEM((nb, 2, GROUP, GROUP), F32)
    if z == 1:
        in_specs = [
            row_spec(RWKV_IN_WIDTH),
            pl.BlockSpec((nb, 8, RWKV_IN_WIDTH),
                         lambda bi, c: (bi, lax.rem(sblk(c) * (SUPER // 8) + n8 - 1, n8), 0)),
            pl.BlockSpec((nb, 8, RWKV_IN_WIDTH),
                         lambda bi, c: (bi, lax.rem((sblk(c) + 1) * (SUPER // 8), n8), 0)),
            const((2, RWKV_IN_WIDTH)), vec, lora, vec, lora, vec, vec,
        ]
        args = [pr, pr, pr, lp["mu"], lp["w0"][z], lp["w1"][z], lp["a0"][z], lp["w2"][z], lp["k_k"], lp["k_a"]]
        out_specs = [row_spec(RWKV_WIDTH), row_spec(RWKV_IN_WIDTH), row_spec(RWKV_WIDTH)]
        out_shape = [jax.ShapeDtypeStruct((B, Tp, RWKV_WIDTH), F32),
                     jax.ShapeDtypeStruct((B, Tp, RWKV_IN_WIDTH), F32),
                     jax.ShapeDtypeStruct((B, Tp, RWKV_WIDTH), F32)]
        scratch = [state, pltpu.VMEM((nb, SUPER + 16, RWKV_IN_WIDTH), F32)]
    else:
        o_b, ps, kkn = o_bwd
        in_specs = [row_spec(RWKV_IN_WIDTH), row_spec(RWKV_WIDTH), row_spec(RWKV_WIDTH),
                    vec, lora, vec, lora, vec, lora, lora, vec, vec, vec, vec]
        args = [ps, kkn, o_b, lp["w0"][z], lp["w1"][z], lp["a0"][z], lp["w2"][z], lp["a0"][1], lp["w2"][1],
                lp["w3"], lp["k_a"], lp["r_k"], lp["gn_g"], lp["gn_b"]]
        out_specs = row_spec(RWKV_WIDTH)
        out_shape = jax.ShapeDtypeStruct((B, Tp, RWKV_WIDTH), F32)
        scratch = [state]
    in_specs.append(const((GROUP, GROUP)))
    args.append(ones256)
    return pl.pallas_call(
        functools.partial(_rwkv_kernel, direction=z, n_super=n_super),
        grid=(B // nb, n_super),
        in_specs=in_specs,
        out_specs=out_specs,
        out_shape=out_shape,
        scratch_shapes=scratch,
        compiler_params=pltpu.CompilerParams(
            dimension_semantics=("arbitrary", "arbitrary"), vmem_limit_bytes=VMEM_LIMIT),
        name="rwkv_fwd" if z == 0 else "rwkv_bwd",
    )(*args)


def _rms_rope(x, gain, cos, sin, ones_b):
    ms = _dot3_right(x * x, ones_b) * (1.0 / HEAD_DIM)
    xn = x * lax.rsqrt(ms + QK_EPS) * gain
    lane = lax.broadcasted_iota(jnp.int32, (1, LANES), 1)
    first = (lane % (2 * ROPE_FREQS)) < ROPE_FREQS
    partner = jnp.where(first, pltpu.roll(xn, LANES - ROPE_FREQS, axis=1),
                        pltpu.roll(xn, ROPE_FREQS, axis=1))
    return xn * cos + partner * sin


def _attn_kernel(q_ref, kv_ref, cq_ref, sq_ref, ck_ref, sk_ref, qg_ref, kg_ref, ones_ref,
                 o_ref, kt_ref, vv_ref, *, n_real):
    N = n_real
    i = pl.program_id(1)
    ones_b = ones_ref[...]
    lane = lax.broadcasted_iota(jnp.int32, (1, LANES), 1)
    low = lane < HEAD_DIM

    @pl.when(i == 0)
    def _():
        def put(kn, vv, col0, width):
            kt = kn.T
            zero = jnp.zeros((HEAD_DIM, width), F32)
            cols = pl.ds(col0, width)
            kt_ref[0, :, cols] = jnp.concatenate([kt[:HEAD_DIM], zero], axis=0).astype(BF16)
            kt_ref[1, :, cols] = jnp.concatenate([zero, kt[:HEAD_DIM]], axis=0).astype(BF16)
            kt_ref[2, :, cols] = jnp.concatenate([kt[HEAD_DIM:], zero], axis=0).astype(BF16)
            kt_ref[3, :, cols] = jnp.concatenate([zero, kt[HEAD_DIM:]], axis=0).astype(BF16)
            one_hi = jnp.where(lane == HEAD_DIM, 1.0, 0.0)
            one_lo = jnp.where(lane == 0, 1.0, 0.0)
            vsw = pltpu.roll(vv, HEAD_DIM, axis=1)
            vv_ref[0, cols, :] = jnp.where(low, vv, one_hi).astype(BF16)
            vv_ref[1, cols, :] = jnp.where(low, one_lo, vsw).astype(BF16)
            vv_ref[2, cols, :] = jnp.where(low, vsw, one_hi).astype(BF16)
            vv_ref[3, cols, :] = jnp.where(low, one_lo, vv).astype(BF16)

        ck = min(K_PREP_ROWS, N)

        def body(rb, carry):
            r0 = pl.multiple_of(rb * ck, ck)
            rows = pl.ds(r0, ck)
            kvc = kv_ref[0, rows, :]
            kn = _rms_rope(kvc[:, :KV_WIDTH], kg_ref[...], ck_ref[rows, :], sk_ref[rows, :], ones_b)
            put(kn, kvc[:, KV_WIDTH:], r0, ck)
            return carry

        lax.fori_loop(0, N // ck, body, 0)
        kvt = kv_ref[0, N:N + TAIL, :]
        knt = _rms_rope(kvt[:, :KV_WIDTH], kg_ref[...], ck_ref[N:N + TAIL, :], sk_ref[N:N + TAIL, :], ones_b)
        put(knt, kvt[:, KV_WIDTH:], N, TAIL)

    tail_col = lax.broadcasted_iota(jnp.int32, (1, TAIL), 1)
    tail_bias = jnp.where(tail_col >= PAD, 0.0, NEG_BIG)

    q = q_ref[0]
    cq = cq_ref[...]
    sq = sq_ref[...]
    heads = [(jp, e) for jp in range(ATTN_WIDTH // LANES) for e in range(2)]
    qp = [(_rms_rope(q[:, jp * LANES:(jp + 1) * LANES], qg_ref[...], cq, sq, ones_b)
           * (HEAD_DIM ** -0.5 * LOG2_E)).astype(BF16) for jp in range(ATTN_WIDTH // LANES)]

    def scores(jp, e):
        kidx = 2 * ((2 * jp + e) // 4) + e
        return jnp.concatenate(
            [_dot(qp[jp], kt_ref[kidx, :, :N]), _dot(qp[jp], kt_ref[kidx, :, N:]) + tail_bias], axis=1)

    def weights(s):
        return jnp.exp2(s - jnp.max(s, axis=-1, keepdims=True)).astype(BF16)

    n_heads = len(heads)
    s = {0: scores(*heads[0]), 1: scores(*heads[1])}
    p = {0: weights(s.pop(0))}
    halves = {}
    for n, (jp, e) in enumerate(heads):
        if n + 2 < n_heads:
            s[n + 2] = scores(*heads[n + 2])
        if n + 1 < n_heads:
            p[n + 1] = weights(s.pop(n + 1))
        vidx = 2 * ((2 * jp + e) // 4) + e
        acc = _dot(p.pop(n), vv_ref[vidx])
        ones_lane = (1 - e) * HEAD_DIM
        halves[(jp, e)] = acc / acc[:, ones_lane:ones_lane + 1]
        if e == 1:
            o_ref[0, :, jp * LANES:(jp + 1) * LANES] = jnp.where(low, halves[(jp, 0)], halves[(jp, 1)])


def _attention(q, kv, cos_t, sin_t, q_gain, k_gain, ones128):
    B, Tp, _ = q.shape
    N = Tp - TAIL
    tq = min(Q_TILE, N)
    nq = N // tq
    nk = N + TAIL
    return pl.pallas_call(
        functools.partial(_attn_kernel, n_real=N),
        grid=(B, nq + 1),
        in_specs=[
            pl.BlockSpec((1, tq, ATTN_WIDTH), lambda bi, i: (bi, i, 0)),
            pl.BlockSpec((1, Tp, 2 * KV_WIDTH), lambda bi, i: (bi, 0, 0)),
            pl.BlockSpec((tq, LANES), lambda bi, i: (i, 0)),
            pl.BlockSpec((tq, LANES), lambda bi, i: (i, 0)),
            pl.BlockSpec((Tp, LANES), lambda bi, i: (0, 0)),
            pl.BlockSpec((Tp, LANES), lambda bi, i: (0, 0)),
            pl.BlockSpec((1, LANES), lambda bi, i: (0, 0)),
            pl.BlockSpec((1, LANES), lambda bi, i: (0, 0)),
            pl.BlockSpec((LANES, LANES), lambda bi, i: (0, 0)),
        ],
        out_specs=pl.BlockSpec((1, tq, ATTN_WIDTH), lambda bi, i: (bi, i, 0)),
        out_shape=jax.ShapeDtypeStruct((B, Tp, ATTN_WIDTH), F32),
        scratch_shapes=[
            pltpu.VMEM((4, LANES, nk), BF16),
            pltpu.VMEM((4, nk, LANES), BF16),
        ],
        compiler_params=pltpu.CompilerParams(
            dimension_semantics=("arbitrary", "arbitrary"), vmem_limit_bytes=VMEM_LIMIT),
        name="attention",
    )(q, kv, cos_t, sin_t, cos_t, sin_t, q_gain, k_gain, ones128)


def _post_kernel(h_ref, yr_ref, ya_ref, wor_ref, woa_ref, g1_ref, b1_ref,
                 wg_ref, wu_ref, wd_ref, g2_ref, b2_ref, o_ref):
    mix = _dot(yr_ref[...].astype(BF16), wor_ref[...]) + _dot(ya_ref[...].astype(BF16), woa_ref[...])
    x1 = _layer_norm(DEEPNORM_ALPHA * h_ref[...] + mix, g1_ref[...], b1_ref[...])
    x1b = x1.astype(BF16)
    ffn = None
    for jc in range(D_FF // FF_CHUNK):
        cols = slice(jc * FF_CHUNK, (jc + 1) * FF_CHUNK)
        gate = _dot(x1b, wg_ref[:, cols])
        up = _dot(x1b, wu_ref[:, cols])
        half = 0.5 * gate
        act = ((half + half * jnp.tanh(half)) * up).astype(BF16)
        part = _dot(act, wd_ref[cols, :])
        ffn = part if ffn is None else ffn + part
    o_ref[...] = _layer_norm(DEEPNORM_ALPHA * x1 + ffn, g2_ref[...], b2_ref[...])


def _post(h2, yr2, ya2, lp):
    M, D = h2.shape
    tm = _pick_tile(M, ROW_TILE)

    def const(shape):
        return pl.BlockSpec(shape, lambda i: (0, 0), pipeline_mode=pl.Buffered(1))

    return pl.pallas_call(
        _post_kernel,
        grid=(M // tm,),
        in_specs=[
            pl.BlockSpec((tm, D), lambda i: (i, 0)),
            pl.BlockSpec((tm, RWKV_WIDTH), lambda i: (i, 0)),
            pl.BlockSpec((tm, ATTN_WIDTH), lambda i: (i, 0)),
            const((RWKV_WIDTH, D)),
            const((ATTN_WIDTH, D)),
            const((1, D)),
            const((1, D)),
            const((D, D_FF)),
            const((D, D_FF)),
            const((D_FF, D)),
            const((1, D)),
            const((1, D)),
        ],
        out_specs=pl.BlockSpec((tm, D), lambda i: (i, 0)),
        out_shape=jax.ShapeDtypeStruct((M, D), F32),
        compiler_params=pltpu.CompilerParams(
            dimension_semantics=("arbitrary",), vmem_limit_bytes=VMEM_LIMIT),
        name="post",
    )(h2, yr2, ya2, lp["wo_r"], lp["wo_a"], lp["ln1_g"], lp["ln1_b"],
      lp["w_gate"], lp["w_up"], lp["w_down"], lp["ln2_g"], lp["ln2_b"])


def _rope_tables(n_real):
    tok = jnp.arange(n_real)
    pos = jnp.stack([tok // GRID_W, tok % GRID_W], axis=-1).astype(F32)
    inv_freq = ROPE_THETA ** (-jnp.arange(ROPE_FREQS, dtype=F32) / ROPE_FREQS)
    ang = pos[:, :, None] * inv_freq
    ang = jnp.concatenate([ang, jnp.zeros((TAIL, 2, ROPE_FREQS), F32)], axis=0)
    cos, sin = jnp.cos(ang), jnp.sin(ang)
    cos64 = jnp.concatenate([cos[:, 0], cos[:, 0], cos[:, 1], cos[:, 1]], axis=-1)
    sin64 = jnp.concatenate([-sin[:, 0], sin[:, 0], -sin[:, 1], sin[:, 1]], axis=-1)
    return jnp.tile(cos64, (1, 2)), jnp.tile(sin64, (1, 2))


def _block_ones(width):
    idx = jnp.arange(width) // HEAD_DIM
    return (idx[:, None] == idx[None, :]).astype(BF16)


def _layer_params(l, w_in, shift_mu, decay_w0, decay_up, iclr_a0, iclr_up, gate_up, k_k, k_a, r_k,
                  gn_g, gn_b, q_gain, k_gain, w_out, ln1_g, ln1_b, w_ffn_in, w_ffn_out, ln2_g, ln2_b):
    def lora(up, row0):
        rows = up.shape[-2]
        full = jnp.zeros(up.shape[:-2] + (LORA_WIDTH, RWKV_WIDTH), F32)
        return full.at[..., row0:row0 + rows, :].set(up).astype(BF16)

    row = lambda a: a.reshape(1, -1)
    return dict(
        w_in=w_in[l].astype(BF16),
        mu=shift_mu[l],
        w0=decay_w0[l].reshape(2, 1, RWKV_WIDTH),
        w1=lora(decay_up[l], 0),
        a0=iclr_a0[l].reshape(2, 1, RWKV_WIDTH),
        w2=lora(iclr_up[l], 32),
        w3=lora(gate_up[l], 64),
        k_k=row(k_k[l]), k_a=row(k_a[l]), r_k=row(r_k[l]), gn_g=row(gn_g[l]), gn_b=row(gn_b[l]),
        q_gain=jnp.tile(row(q_gain[l]), (1, 2)), k_gain=jnp.tile(row(k_gain[l]), (1, 2)),
        wo_r=w_out[l, :RWKV_WIDTH].astype(BF16), wo_a=w_out[l, RWKV_WIDTH:].astype(BF16),
        ln1_g=row(ln1_g[l]), ln1_b=row(ln1_b[l]),
        w_gate=w_ffn_in[l, :, :D_FF].astype(BF16), w_up=w_ffn_in[l, :, D_FF:].astype(BF16),
        w_down=w_ffn_out[l].astype(BF16),
        ln2_g=row(ln2_g[l]), ln2_b=row(ln2_b[l]),
    )


def _trunk(x, meta_tokens, ln_in_g, ln_in_b, layers, ones256, ones128):
    B, N, D = x.shape
    Tp = N + TAIL
    cos_t, sin_t = _rope_tables(N)
    h = _embed(x, meta_tokens, ln_in_g, ln_in_b)
    for lp in layers:
        pr, q, kv = _proj(h.reshape(B * Tp, D), lp["w_in"], Tp)
        pr = pr.reshape(B, Tp, RWKV_IN_WIDTH)
        o_bwd = _rwkv_dir(pr, lp, ones256, 1)
        y_rwkv = _rwkv_dir(pr, lp, ones256, 0, o_bwd)
        y_attn = _attention(q.reshape(B, Tp, ATTN_WIDTH), kv.reshape(B, Tp, 2 * KV_WIDTH),
                            cos_t, sin_t, lp["q_gain"], lp["k_gain"], ones128)
        h = _post(h.reshape(B * Tp, D), y_rwkv.reshape(B * Tp, RWKV_WIDTH),
                  y_attn.reshape(B * Tp, ATTN_WIDTH), lp).reshape(B, Tp, D)
    return h[:, :N]


def kernel(x_prompt, x_sample, meta_tokens, ln_in_g, ln_in_b, w_in, shift_mu, decay_w0, decay_up,
           iclr_a0, iclr_up, gate_up, k_k, k_a, r_k, gn_g, gn_b, q_gain, k_gain, w_out,
           ln1_g, ln1_b, w_ffn_in, w_ffn_out, ln2_g, ln2_b):
    layers = [
        _layer_params(l, w_in, shift_mu, decay_w0, decay_up, iclr_a0, iclr_up, gate_up, k_k, k_a, r_k,
                      gn_g, gn_b, q_gain, k_gain, w_out, ln1_g, ln1_b, w_ffn_in, w_ffn_out, ln2_g, ln2_b)
        for l in range(w_in.shape[0])
    ]
    ones256 = _block_ones(GROUP)
    ones128 = _block_ones(LANES)
    y_prompt = _trunk(x_prompt, meta_tokens, ln_in_g, ln_in_b, layers, ones256, ones128)
    y_sample = _trunk(x_sample, meta_tokens, ln_in_g, ln_in_b, layers, ones256, ones128)
    return (y_prompt, y_sample)
```
